```python
import math
import jax
import jax.numpy as jnp
from jax import lax
import numpy as np


D_MODEL = 4096
BATCH = 2
SEQ = 4096
DEPTH = 2

ROPE_THETA = 500000.0
NORM_EPS = 1e-6
Q_BLOCK = 128
ROPE_FRACTION = 4

MLA_HEADS = 16
MLA_NOPE = 128
MLA_ROPE = 64
MLA_V = 128
MLA_Q_RANK = 1024
MLA_KV_RANK = 512

DIFF_HEADS = 8
DIFF_DIM = 128

DSA_HEADS = 16
DSA_DIM = 128
IDX_HEADS = 16
IDX_DIM = 64
TOPK_MAX = 256

FOX_HEADS = 16
FOX_DIM = 128

MLA_WIDTH = MLA_HEADS * MLA_V
DIFF_WIDTH = DIFF_HEADS * 2 * DIFF_DIM
DSA_WIDTH = DSA_HEADS * DSA_DIM
FOX_WIDTH = FOX_HEADS * FOX_DIM

EVEN_COLS = (MLA_Q_RANK, MLA_KV_RANK, MLA_ROPE, MLA_WIDTH, DIFF_WIDTH, DIFF_WIDTH, DIFF_WIDTH, DIFF_WIDTH)
ODD_COLS = (DSA_WIDTH, DSA_DIM, DSA_DIM, IDX_HEADS * IDX_DIM, IDX_DIM, IDX_HEADS, DSA_WIDTH, FOX_WIDTH, FOX_WIDTH, FOX_WIDTH, FOX_HEADS, FOX_WIDTH)

N_EVEN = (DEPTH + 1) // 2
N_ODD = DEPTH // 2

kernel_name = 'hybrid_mla_diff_dsa_fox_trunk'


def rms_norm(x, g):
    xf = x.astype(jnp.float32)
    y = xf * lax.rsqrt(jnp.mean(xf * xf, axis=-1, keepdims=True) + NORM_EPS)
    return (y * g.astype(jnp.float32)).astype(x.dtype)


def split_cols(a, sizes):
    pts = []
    acc = 0
    for s in sizes[:-1]:
        acc += s
        pts.append(acc)
    return jnp.split(a, pts, axis=-1)


def rope_tables(seq, rot_dim):
    inv = ROPE_THETA ** (-jnp.arange(0, rot_dim, 2, dtype=jnp.float32) / rot_dim)
    ang = jnp.arange(seq, dtype=jnp.float32)[:, None] * inv[None, :]
    return jnp.cos(ang), jnp.sin(ang)


def apply_rope(x, cos, sin):
    half = cos.shape[-1]
    rot = 2 * half
    shape = (1, x.shape[1]) + (1,) * (x.ndim - 3) + (half,)
    c = cos.reshape(shape).astype(x.dtype)
    s = sin.reshape(shape).astype(x.dtype)
    x1 = x[..., :half]
    x2 = x[..., half:rot]
    return jnp.concatenate([x1 * c - x2 * s, x2 * c + x1 * s, x[..., rot:]], axis=-1)


def causal_mask(start, seq):
    qpos = start + jnp.arange(Q_BLOCK)
    kpos = jnp.arange(seq)
    return kpos[None, :] <= qpos[:, None]


def masked_softmax(s, mask):
    return jax.nn.softmax(jnp.where(mask, s, -jnp.inf), axis=-1)


def sweep_query_blocks(fn, q_inputs):
    b, s = q_inputs[0].shape[:2]
    nb = s // Q_BLOCK
    blocks = tuple(jnp.moveaxis(a.reshape((b, nb, Q_BLOCK) + a.shape[2:]), 1, 0) for a in q_inputs)
    starts = jnp.arange(nb, dtype=jnp.int32) * Q_BLOCK
    out = lax.map(lambda args: fn(args[0], *args[1]), (starts, blocks))
    out = jnp.moveaxis(out, 0, 1)
    return out.reshape((b, s) + out.shape[3:])


def even_layer(x, norm_g, w_in, q_norm_g, w_uq, kv_norm_g, w_ukv, diff_lambda, subln_g, w_out, lambda_init):
    B, S, _ = x.shape
    h = rms_norm(x, norm_g)
    c_q, c_kv, k_r, gate_a, dq, dk, dv, gate_b = split_cols(h @ w_in, EVEN_COLS)

    cos_a, sin_a = rope_tables(S, MLA_ROPE)
    q = (rms_norm(c_q, q_norm_g) @ w_uq).reshape(B, S, MLA_HEADS, MLA_NOPE + MLA_ROPE)
    q_nope = q[..., :MLA_NOPE]
    q_rope = apply_rope(q[..., MLA_NOPE:], cos_a, sin_a)
    kv = (rms_norm(c_kv, kv_norm_g) @ w_ukv).reshape(B, S, MLA_HEADS, MLA_NOPE + MLA_V)
    k_nope = kv[..., :MLA_NOPE]
    v_a = kv[..., MLA_NOPE:]
    k_rope = apply_rope(k_r, cos_a, sin_a)
    mla_scale = (MLA_NOPE + MLA_ROPE) ** -0.5

    def mla_block(start, qn, qr):
        s = jnp.einsum('bqhd,bkhd->bhqk', qn, k_nope) + jnp.einsum('bqhr,bkr->bhqk', qr, k_rope)
        p = masked_softmax(s.astype(jnp.float32) * mla_scale, causal_mask(start, S))
        return jnp.einsum('bhqk,bkhd->bqhd', p.astype(v_a.dtype), v_a)

    o_a = sweep_query_blocks(mla_block, (q_nope, q_rope)).reshape(B, S, MLA_WIDTH)

    cos_b, sin_b = rope_tables(S, DIFF_DIM // ROPE_FRACTION)
    dq = apply_rope(dq.reshape(B, S, DIFF_HEADS, 2, DIFF_DIM), cos_b, sin_b)
    dk = apply_rope(dk.reshape(B, S, DIFF_HEADS, 2, DIFF_DIM), cos_b, sin_b)
    dv = dv.reshape(B, S, DIFF_HEADS, 2 * DIFF_DIM)
    lp = diff_lambda.astype(jnp.float32)
    lam = jnp.exp(jnp.sum(lp[0] * lp[1])) - jnp.exp(jnp.sum(lp[2] * lp[3])) + lambda_init
    diff_scale = DIFF_DIM ** -0.5

    def diff_block(start, qb):
        s = jnp.einsum('bqhcd,bkhcd->bchqk', qb, dk).astype(jnp.float32) * diff_scale
        p = masked_softmax(s, causal_mask(start, S))
        a = p[:, 0] - lam * p[:, 1]
        return jnp.einsum('bhqk,bkhe->bqhe', a.astype(dv.dtype), dv)

    o_b = sweep_query_blocks(diff_block, (dq,))
    o_b = (rms_norm(o_b, subln_g) * (1.0 - lambda_init)).reshape(B, S, DIFF_WIDTH)

    mixed = jnp.concatenate([o_a * jax.nn.silu(gate_a), o_b * jax.nn.silu(gate_b)], axis=-1)
    return mixed @ w_out


def odd_layer(x, norm_g, w_in, forget_bias, w_out):
    B, S, _ = x.shape
    h = rms_norm(x, norm_g)
    (dsa_q, dsa_k, dsa_v, idx_q, idx_k, idx_w, gate_c,
     fox_q, fox_k, fox_v, fox_f, gate_d) = split_cols(h @ w_in, ODD_COLS)

    cos_c, sin_c = rope_tables(S, DSA_DIM // ROPE_FRACTION)
    cos_i, sin_i = rope_tables(S, IDX_DIM // ROPE_FRACTION)
    q_c = apply_rope(dsa_q.reshape(B, S, DSA_HEADS, DSA_DIM), cos_c, sin_c)
    k_c = apply_rope(dsa_k, cos_c, sin_c)
    v_c = dsa_v
    q_i = apply_rope(idx_q.reshape(B, S, IDX_HEADS, IDX_DIM), cos_i, sin_i)
    k_i = apply_rope(idx_k, cos_i, sin_i)
    w_i = idx_w * ((IDX_HEADS ** -0.5) * (IDX_DIM ** -0.5))
    top_k = min(TOPK_MAX, S // 4)
    dsa_scale = DSA_DIM ** -0.5
    gather_rows = jax.vmap(lambda src, ids: src[ids])

    def dsa_block(start, qb, qib, wib):
        qpos = start + jnp.arange(Q_BLOCK)
        rel = jax.nn.relu(jnp.einsum('bqhd,bkd->bqhk', qib, k_i))
        idx_score = jnp.einsum('bqhk,bqh->bqk', rel, wib).astype(jnp.float32)
        idx_score = jnp.where(causal_mask(start, S), idx_score, -jnp.inf)
        _, sel = lax.top_k(idx_score, top_k)
        valid = sel <= qpos[None, :, None]
        k_sel = gather_rows(k_c, sel)
        v_sel = gather_rows(v_c, sel)
        s = jnp.einsum('bqhd,bqkd->bhqk', qb, k_sel).astype(jnp.float32) * dsa_scale
        p = masked_softmax(s, valid[:, None])
        return jnp.einsum('bhqk,bqkd->bqhd', p.astype(v_sel.dtype), v_sel)

    o_c = sweep_query_blocks(dsa_block, (q_c, q_i, w_i)).reshape(B, S, DSA_WIDTH)

    fq = fox_q.reshape(B, S, FOX_HEADS, FOX_DIM)
    fk = fox_k.reshape(B, S, FOX_HEADS, FOX_DIM)
    fv = fox_v.reshape(B, S, FOX_HEADS, FOX_DIM)
    log_f = jax.nn.log_sigmoid(fox_f.astype(jnp.float32) + forget_bias.astype(jnp.float32))
    cum = jnp.cumsum(log_f, axis=1)
    cum_k = jnp.moveaxis(cum, 1, 2)[:, :, None, :]
    fox_scale = FOX_DIM ** -0.5

    def fox_block(start, qb, cqb):
        s = jnp.einsum('bqhd,bkhd->bhqk', qb, fk).astype(jnp.float32) * fox_scale
        s = s + (jnp.moveaxis(cqb, 1, 2)[..., None] - cum_k)
        p = masked_softmax(s, causal_mask(start, S))
        return jnp.einsum('bhqk,bkhd->bqhd', p.astype(fv.dtype), fv)

    o_d = sweep_query_blocks(fox_block, (fq, cum)).reshape(B, S, FOX_WIDTH)

    mixed = jnp.concatenate([o_c * jax.nn.silu(gate_c), o_d * jax.nn.silu(gate_d)], axis=-1)
    return mixed @ w_out


def setup_inputs(seed: int = 0) -> dict:
    key = jax.random.key(seed)
    ks = jax.random.split(key, 16)
    f32 = jnp.float32

    def normal(k, shape, scale):
        return jax.random.normal(k, shape, f32) * scale

    def gain(k, shape):
        return 1.0 + 0.01 * jax.random.normal(k, shape, f32)

    even_in = sum(EVEN_COLS)
    odd_in = sum(ODD_COLS)
    even_mix = MLA_WIDTH + DIFF_WIDTH
    odd_mix = DSA_WIDTH + FOX_WIDTH
    return {
        'x': jax.random.normal(ks[0], (BATCH, SEQ, D_MODEL), f32),
        'even_norm': gain(ks[1], (N_EVEN, D_MODEL)),
        'even_w_in': normal(ks[2], (N_EVEN, D_MODEL, even_in), D_MODEL ** -0.5),
        'mla_q_norm': gain(ks[3], (N_EVEN, MLA_Q_RANK)),
        'mla_w_uq': normal(ks[4], (N_EVEN, MLA_Q_RANK, MLA_HEADS * (MLA_NOPE + MLA_ROPE)), MLA_Q_RANK ** -0.5),
        'mla_kv_norm': gain(ks[5], (N_EVEN, MLA_KV_RANK)),
        'mla_w_ukv': normal(ks[6], (N_EVEN, MLA_KV_RANK, MLA_HEADS * (MLA_NOPE + MLA_V)), MLA_KV_RANK ** -0.5),
        'diff_lambda': normal(ks[7], (N_EVEN, 4, DIFF_DIM), 0.1),
        'diff_subln': gain(ks[8], (N_EVEN, 2 * DIFF_DIM)),
        'even_w_out': normal(ks[9], (N_EVEN, even_mix, D_MODEL), even_mix ** -0.5),
        'odd_norm': gain(ks[10], (N_ODD, D_MODEL)),
        'odd_w_in': normal(ks[11], (N_ODD, D_MODEL, odd_in), D_MODEL ** -0.5),
        'fox_forget_bias': jax.random.uniform(ks[12], (N_ODD, FOX_HEADS), f32, 1.0, 4.0),
        'odd_w_out': normal(ks[13], (N_ODD, odd_mix, D_MODEL), odd_mix ** -0.5),
        'final_norm': gain(ks[14], (D_MODEL,)),
    }


def reference(x, even_norm, even_w_in, mla_q_norm, mla_w_uq, mla_kv_norm, mla_w_ukv, diff_lambda, diff_subln, even_w_out, odd_norm, odd_w_in, fox_forget_bias, odd_w_out, final_norm):
    h = x
    for layer in range(DEPTH):
        i = layer // 2
        if layer % 2 == 0:
            lambda_init = 0.8 - 0.6 * math.exp(-0.3 * layer)
            h = h + even_layer(h, even_norm[i], even_w_in[i], mla_q_norm[i], mla_w_uq[i], mla_kv_norm[i], mla_w_ukv[i], diff_lambda[i], diff_subln[i], even_w_out[i], lambda_init)
        else:
            h = h + odd_layer(h, odd_norm[i], odd_w_in[i], fox_forget_bias[i], odd_w_out[i])
    return rms_norm(h, final_norm)
```

```python
import functools
import math

import jax
import jax.numpy as jnp
from jax import lax
from jax.experimental import pallas as pl
from jax.experimental.pallas import tpu as pltpu

F32 = jnp.float32
BF16 = jnp.bfloat16

ROPE_THETA = 500000.0
NORM_EPS = 1e-6
ROPE_FRACTION = 4

MLA_HEADS, MLA_NOPE, MLA_ROPE, MLA_V = 16, 128, 64, 128
MLA_Q_RANK, MLA_KV_RANK = 1024, 512
DIFF_HEADS, DIFF_DIM = 8, 128
DSA_HEADS, DSA_DIM = 16, 128
IDX_HEADS, IDX_DIM = 16, 64
TOPK_MAX = 256
FOX_HEADS, FOX_DIM = 16, 128

MLA_WIDTH = MLA_HEADS * MLA_V
DIFF_WIDTH = DIFF_HEADS * 2 * DIFF_DIM
DSA_WIDTH = DSA_HEADS * DSA_DIM
FOX_WIDTH = FOX_HEADS * FOX_DIM

LANES = 128
VMEM_LIMIT_BYTES = 56 * 1024 * 1024
MASKED = -1e30

_NT = (((1,), (1,)), ((), ()))


def _params(*sem):
    return pltpu.CompilerParams(dimension_semantics=sem, vmem_limit_bytes=VMEM_LIMIT_BYTES)


def _rmsnorm_kernel(x_ref, g_ref, o_ref):
    x = x_ref[...].astype(F32)
    y = x * lax.rsqrt(jnp.mean(x * x, axis=-1, keepdims=True) + NORM_EPS)
    o_ref[...] = (y * g_ref[...]).astype(o_ref.dtype)


def _rmsnorm(x, g, out_dtype, tm=256):
    t, d = x.shape
    return pl.pallas_call(
        _rmsnorm_kernel,
        grid=(t // tm,),
        in_specs=[pl.BlockSpec((tm, d), lambda i: (i, 0)), pl.BlockSpec((1, d), lambda i: (0, 0))],
        out_specs=pl.BlockSpec((tm, d), lambda i: (i, 0)),
        out_shape=jax.ShapeDtypeStruct((t, d), out_dtype),
        compiler_params=_params("arbitrary"),
        name="rmsnorm",
    )(x, g.reshape(1, d).astype(F32))


def _mm_kernel(a_ref, b_ref, o_ref, acc_ref, *, nk):
    k = pl.program_id(2)

    @pl.when(k == 0)
    def _():
        acc_ref[...] = jnp.zeros_like(acc_ref)

    acc_ref[...] += jnp.dot(a_ref[...], b_ref[...], preferred_element_type=F32)

    @pl.when(k == nk - 1)
    def _():
        o_ref[...] = acc_ref[...].astype(o_ref.dtype)


def _matmul(a, b, out_dtype, *, tm, tn, tk, name):
    m, kdim = a.shape
    n = b.shape[1]
    nk = kdim // tk
    return pl.pallas_call(
        functools.partial(_mm_kernel, nk=nk),
        grid=(m // tm, n // tn, nk),
        in_specs=[pl.BlockSpec((tm, tk), lambda i, j, k: (i, k)),
                  pl.BlockSpec((tk, tn), lambda i, j, k: (k, j))],
        out_specs=pl.BlockSpec((tm, tn), lambda i, j, k: (i, j)),
        out_shape=jax.ShapeDtypeStruct((m, n), out_dtype),
        scratch_shapes=[pltpu.VMEM((tm, tn), F32)],
        compiler_params=_params("arbitrary", "arbitrary", "arbitrary"),
        name=name,
    )(a, b)


def _norm_mm_kernel(a_ref, g_ref, b_ref, o_ref):
    x = a_ref[...].astype(F32)
    y = x * lax.rsqrt(jnp.mean(x * x, axis=-1, keepdims=True) + NORM_EPS) * g_ref[...]
    o_ref[...] = jnp.dot(y.astype(BF16), b_ref[...], preferred_element_type=F32).astype(o_ref.dtype)


def _norm_matmul(a, a_col_block, g, b, *, tm, tn, name):
    m = a.shape[0]
    kdim, n = b.shape
    return pl.pallas_call(
        _norm_mm_kernel,
        grid=(m // tm, n // tn),
        in_specs=[pl.BlockSpec((tm, kdim), lambda i, j: (i, a_col_block)),
                  pl.BlockSpec((1, kdim), lambda i, j: (0, 0)),
                  pl.BlockSpec((kdim, tn), lambda i, j: (0, j))],
        out_specs=pl.BlockSpec((tm, tn), lambda i, j: (i, j)),
        out_shape=jax.ShapeDtypeStruct((m, n), BF16),
        compiler_params=_params("arbitrary", "arbitrary"),
        name=name,
    )(a, g.reshape(1, kdim).astype(F32), b)


def _outproj_kernel(o_ref, gate_ref, w_ref, res_ref, h_ref, acc_ref, *, nk):
    k = pl.program_id(2)

    @pl.when(k == 0)
    def _():
        acc_ref[...] = jnp.zeros_like(acc_ref)

    gate = gate_ref[...].astype(F32)
    a = o_ref[...].astype(F32) * (gate * jax.nn.sigmoid(gate))
    acc_ref[...] += jnp.dot(a.astype(BF16), w_ref[...], preferred_element_type=F32)

    @pl.when(k == nk - 1)
    def _():
        h_ref[...] = acc_ref[...] + res_ref[...]


def _outproj(o, proj, w, res, *, tm, tn, tk):
    m, kdim = o.shape
    n = w.shape[1]
    nk = kdim // tk
    return pl.pallas_call(
        functools.partial(_outproj_kernel, nk=nk),
        grid=(m // tm, n // tn, nk),
        in_specs=[pl.BlockSpec((tm, tk), lambda i, j, k: (i, k)),
                  pl.BlockSpec((tm, tk), lambda i, j, k: (i, k)),
                  pl.BlockSpec((tk, tn), lambda i, j, k: (k, j)),
                  pl.BlockSpec((tm, tn), lambda i, j, k: (i, j))],
        out_specs=pl.BlockSpec((tm, tn), lambda i, j, k: (i, j)),
        out_shape=jax.ShapeDtypeStruct((m, n), F32),
        scratch_shapes=[pltpu.VMEM((tm, tn), F32)],
        compiler_params=_params("arbitrary", "arbitrary", "arbitrary"),
        name="outproj",
    )(o, proj, w, res)


def _rope_tables(seq, rot_dim, period):
    half = rot_dim // 2
    inv = ROPE_THETA ** (-jnp.arange(0, rot_dim, 2, dtype=F32) / rot_dim)
    ang = jnp.arange(seq, dtype=F32)[:, None] * inv[None, :]
    cos, sin = jnp.cos(ang), jnp.sin(ang)
    ones = jnp.ones((seq, period - rot_dim), F32)
    c = jnp.concatenate([cos, cos, ones], axis=1)
    s = jnp.concatenate([-sin, sin, 0.0 * ones], axis=1)
    reps = LANES // period
    return jnp.tile(c, (1, reps)), jnp.tile(s, (1, reps))


def _rope(x, c, s, half, period):
    lane = lax.broadcasted_iota(jnp.int32, x.shape, 1)
    first = (lane & (period - 1)) < half
    partner = jnp.where(first, pltpu.roll(x, LANES - half, axis=1), pltpu.roll(x, half, axis=1))
    return x * c + partner * s


def _online_step(q, k, v, carry, scale, bias=None, mask=None):
    m, l, acc = carry
    s = lax.dot_general(q, k, _NT, preferred_element_type=F32) * scale
    if bias is not None:
        s = s + bias
    if mask is not None:
        s = jnp.where(mask, s, MASKED)
    m_new = jnp.maximum(m, jnp.max(s, axis=1, keepdims=True))
    alpha = jnp.exp(m - m_new)
    p = jnp.exp(s - m_new)
    l = alpha * l + jnp.sum(p, axis=1, keepdims=True)
    acc = alpha * acc + jnp.dot(p.astype(BF16), v, preferred_element_type=F32)
    return m_new, l, acc


def _init_carry(rows, dv):
    return (jnp.full((rows, 1), MASKED, F32), jnp.zeros((rows, 1), F32), jnp.zeros((rows, dv), F32))


def _diag_mask(bq):
    row = lax.broadcasted_iota(jnp.int32, (bq, bq), 0)
    col = lax.broadcasted_iota(jnp.int32, (bq, bq), 1)
    return row >= col


def _mla_kernel(q_ref, kn_ref, kr_ref, v_ref, cq_ref, sq_ref, ck_ref, sk_ref, o_ref, k_scr, *, bq, scale):
    i = pl.program_id(2)
    half = MLA_ROPE // 2

    @pl.when(i == 0)
    def _():
        k_scr[:, :LANES] = kn_ref[...]
        k_scr[:, LANES:] = _rope(kr_ref[...].astype(F32), ck_ref[...], sk_ref[...], half, LANES).astype(BF16)

    qr = _rope(q_ref[:, LANES:].astype(F32), cq_ref[...], sq_ref[...], half, LANES).astype(BF16)
    q = jnp.concatenate([q_ref[:, :LANES], qr], axis=1)

    def step(j, carry):
        off = pl.multiple_of(j * bq, bq)
        return _online_step(q, k_scr[pl.ds(off, bq), :], v_ref[pl.ds(off, bq), :], carry, scale)

    carry = lax.fori_loop(0, i, step, _init_carry(bq, MLA_V))
    off = pl.multiple_of(i * bq, bq)
    _, l, acc = _online_step(q, k_scr[pl.ds(off, bq), :], v_ref[pl.ds(off, bq), :], carry, scale,
                             mask=_diag_mask(bq))
    o_ref[...] = (acc / l).astype(o_ref.dtype)


def _mla_attention(q, kv, proj_a, tabs, batch, seq, *, bq):
    t = batch * seq
    nq = seq // bq
    c_tab, s_tab = tabs
    kr_block = (MLA_Q_RANK + MLA_KV_RANK) // LANES
    return pl.pallas_call(
        functools.partial(_mla_kernel, bq=bq, scale=(MLA_NOPE + MLA_ROPE) ** -0.5),
        grid=(batch, MLA_HEADS, nq),
        in_specs=[pl.BlockSpec((bq, 2 * LANES), lambda b, h, i: (b * nq + i, h)),
                  pl.BlockSpec((seq, LANES), lambda b, h, i: (b, h)),
                  pl.BlockSpec((seq, LANES), lambda b, h, i: (b, kr_block)),
                  pl.BlockSpec((seq, LANES), lambda b, h, i: (b, MLA_HEADS + h)),
                  pl.BlockSpec((bq, LANES), lambda b, h, i: (i, 0)),
                  pl.BlockSpec((bq, LANES), lambda b, h, i: (i, 0)),
                  pl.BlockSpec((seq, LANES), lambda b, h, i: (0, 0)),
                  pl.BlockSpec((seq, LANES), lambda b, h, i: (0, 0))],
        out_specs=pl.BlockSpec((bq, MLA_V), lambda b, h, i: (b * nq + i, h)),
        out_shape=jax.ShapeDtypeStruct((t, MLA_WIDTH), BF16),
        scratch_shapes=[pltpu.VMEM((seq, 2 * LANES), BF16)],
        compiler_params=_params("arbitrary", "arbitrary", "arbitrary"),
        name="mla_attention",
    )(q, kv, proj_a, kv, c_tab, s_tab, c_tab, s_tab)


def _diff_kernel(q_ref, k_ref, v_ref, lam_ref, g_ref, cq_ref, sq_ref, ck_ref, sk_ref, o_ref, k_scr,
                 *, bq, scale, lambda_init):
    i = pl.program_id(2)
    half = DIFF_DIM // ROPE_FRACTION // 2

    @pl.when(i == 0)
    def _():
        for c in range(2):
            sl = slice(c * LANES, (c + 1) * LANES)
            k_scr[:, sl] = _rope(k_ref[:, sl].astype(F32), ck_ref[...], sk_ref[...], half, LANES).astype(BF16)

    qs = [_rope(q_ref[:, c * LANES:(c + 1) * LANES].astype(F32), cq_ref[...], sq_ref[...], half, LANES).astype(BF16)
          for c in range(2)]

    def chunk(off, carries, mask):
        v = v_ref[pl.ds(off, bq), :]
        return tuple(
            _online_step(qs[c], k_scr[pl.ds(off, bq), c * LANES:(c + 1) * LANES], v, carries[c], scale, mask=mask)
            for c in range(2))

    def step(j, carries):
        return chunk(pl.multiple_of(j * bq, bq), carries, None)

    init = (_init_carry(bq, 2 * DIFF_DIM), _init_carry(bq, 2 * DIFF_DIM))
    carries = lax.fori_loop(0, i, step, init)
    carries = chunk(pl.multiple_of(i * bq, bq), carries, _diag_mask(bq))
    (_, l0, acc0), (_, l1, acc1) = carries

    lp = lam_ref[...]
    lam = (jnp.exp(jnp.sum(lp[0:1] * lp[1:2], axis=1, keepdims=True))
           - jnp.exp(jnp.sum(lp[2:3] * lp[3:4], axis=1, keepdims=True)) + lambda_init)
    o = acc0 / l0 - lam * (acc1 / l1)
    o = o * lax.rsqrt(jnp.mean(o * o, axis=-1, keepdims=True) + NORM_EPS) * g_ref[...]
    o_ref[...] = (o * (1.0 - lambda_init)).astype(o_ref.dtype)


def _diff_attention(proj, lam_params, subln_g, tabs, batch, seq, lambda_init, *, bq):
    t = batch * seq
    nq = seq // bq
    c_tab, s_tab = tabs
    w = 2 * DIFF_DIM
    q0, k0, v0 = 4096 // w, 6144 // w, 8192 // w
    return pl.pallas_call(
        functools.partial(_diff_kernel, bq=bq, scale=DIFF_DIM ** -0.5, lambda_init=lambda_init),
        grid=(batch, DIFF_HEADS, nq),
        in_specs=[pl.BlockSpec((bq, w), lambda b, h, i: (b * nq + i, q0 + h)),
                  pl.BlockSpec((seq, w), lambda b, h, i: (b, k0 + h)),
                  pl.BlockSpec((seq, w), lambda b, h, i: (b, v0 + h)),
                  pl.BlockSpec((4, DIFF_DIM), lambda b, h, i: (0, 0)),
                  pl.BlockSpec((1, w), lambda b, h, i: (0, 0)),
                  pl.BlockSpec((bq, LANES), lambda b, h, i: (i, 0)),
                  pl.BlockSpec((bq, LANES), lambda b, h, i: (i, 0)),
                  pl.BlockSpec((seq, LANES), lambda b, h, i: (0, 0)),
                  pl.BlockSpec((seq, LANES), lambda b, h, i: (0, 0))],
        out_specs=pl.BlockSpec((bq, w), lambda b, h, i: (b * nq + i, h)),
        out_shape=jax.ShapeDtypeStruct((t, DIFF_WIDTH), BF16),
        scratch_shapes=[pltpu.VMEM((seq, w), BF16)],
        compiler_params=_params("arbitrary", "arbitrary", "arbitrary"),
        name="diff_attention",
    )(proj, proj, proj, lam_params.astype(F32), subln_g.reshape(1, w).astype(F32), c_tab, s_tab, c_tab, s_tab)


def _fox_kernel(q_ref, k_ref, v_ref, cum_ref, cumt_ref, o_ref, *, bq, scale):
    h = pl.program_id(1)
    i = pl.program_id(2)
    q = q_ref[...]
    cum = cum_ref[...]
    lane = lax.broadcasted_iota(jnp.int32, cum.shape, 1)
    cq = jnp.sum(jnp.where(lane == FOX_F_LANE + h, cum, 0.0), axis=1, keepdims=True)

    def chunk(j, carry, mask):
        off = pl.multiple_of(j * bq, bq)
        bias = cq - cumt_ref[0, pl.ds(j, 1), :]
        return _online_step(q, k_ref[pl.ds(off, bq), :], v_ref[pl.ds(off, bq), :], carry, scale, bias=bias, mask=mask)

    carry = lax.fori_loop(0, i, lambda j, c: chunk(j, c, None), _init_carry(bq, FOX_DIM))
    _, l, acc = chunk(i, carry, _diag_mask(bq))
    o_ref[...] = (acc / l).astype(o_ref.dtype)


def _fox_attention(proj, cum, cum_t, batch, seq, *, bq):
    t = batch * seq
    nq = seq // bq
    q0, k0, v0 = 6144 // LANES, 8192 // LANES, 10240 // LANES
    return pl.pallas_call(
        functools.partial(_fox_kernel, bq=bq, scale=FOX_DIM ** -0.5),
        grid=(batch, FOX_HEADS, nq),
        in_specs=[pl.BlockSpec((bq, LANES), lambda b, h, i: (b * nq + i, q0 + h)),
                  pl.BlockSpec((seq, LANES), lambda b, h, i: (b, k0 + h)),
                  pl.BlockSpec((seq, LANES), lambda b, h, i: (b, v0 + h)),
                  pl.BlockSpec((bq, LANES), lambda b, h, i: (b * nq + i, 0)),
                  pl.BlockSpec((1, nq, bq), lambda b, h, i: (b * FOX_HEADS + h, 0, 0))],
        out_specs=pl.BlockSpec((bq, FOX_DIM), lambda b, h, i: (b * nq + i, h)),
        out_shape=jax.ShapeDtypeStruct((t, FOX_WIDTH), BF16),
        compiler_params=_params("arbitrary", "arbitrary", "arbitrary"),
        name="fox_attention",
    )(proj, proj, proj, cum, cum_t)


def _logf_cumsum_kernel(f_ref, b_ref, o_ref, carry_ref, *, tb):
    @pl.when(pl.program_id(1) == 0)
    def _():
        carry_ref[...] = jnp.zeros_like(carry_ref)

    x = f_ref[...] + b_ref[...]
    logf = jnp.minimum(x, 0.0) - jnp.log1p(jnp.exp(-jnp.abs(x)))
    row = lax.broadcasted_iota(jnp.int32, (tb, tb), 0)
    col = lax.broadcasted_iota(jnp.int32, (tb, tb), 1)
    tri = jnp.where(row >= col, 1.0, 0.0).astype(BF16)
    hi = logf.astype(BF16)
    r1 = logf - hi.astype(F32)
    mid = r1.astype(BF16)
    lo = (r1 - mid.astype(F32)).astype(BF16)
    cum = (jnp.dot(tri, hi, preferred_element_type=F32) + jnp.dot(tri, mid, preferred_element_type=F32)
           + jnp.dot(tri, lo, preferred_element_type=F32)) + carry_ref[...]
    o_ref[...] = cum
    carry_ref[...] = cum[tb - 1:tb, :]


FOX_F_LANE = 2 * DSA_DIM + IDX_DIM + IDX_HEADS - 2 * LANES
IDX_W_LANE = 2 * DSA_DIM + IDX_DIM - 2 * LANES


def _logf_cumsum(small, forget_bias, batch, seq, *, tb=512):
    t = batch * seq
    nb = seq // tb
    bias = jnp.zeros((1, LANES), F32).at[0, FOX_F_LANE:FOX_F_LANE + FOX_HEADS].set(forget_bias.astype(F32))
    return pl.pallas_call(
        functools.partial(_logf_cumsum_kernel, tb=tb),
        grid=(batch, nb),
        in_specs=[pl.BlockSpec((tb, LANES), lambda b, i: (b * nb + i, 2)),
                  pl.BlockSpec((1, LANES), lambda b, i: (0, 0))],
        out_specs=pl.BlockSpec((tb, LANES), lambda b, i: (b * nb + i, 0)),
        out_shape=jax.ShapeDtypeStruct((t, LANES), F32),
        scratch_shapes=[pltpu.VMEM((1, LANES), F32)],
        compiler_params=_params("arbitrary", "arbitrary"),
        name="logf_cumsum",
    )(small, bias)


DSA_CK = 256
DSA_BQ = 128
DSA_MAX_ITERS = 320


def _dsa_kernel(qc_ref, qi_ref, wq_ref, kv_ref, cdq_ref, sdq_ref, cdk_ref, sdk_ref, ciq_ref, siq_ref,
                cik_ref, sik_ref, o_ref, kc_scr, vc_scr, ki_scr, idx_scr, *, seq, topk):
    i = pl.program_id(1)
    bq, ck = DSA_BQ, DSA_CK
    dhalf = DSA_DIM // ROPE_FRACTION // 2
    ihalf = IDX_DIM // ROPE_FRACTION // 2
    rows = DSA_HEADS * bq
    lane = lax.broadcasted_iota(jnp.int32, (bq, LANES), 1)

    @pl.when(i == 0)
    def _():
        tb = 512

        def prep(r, _):
            off = pl.multiple_of(r * tb, tb)
            sl = pl.ds(off, tb)
            kc_scr[sl, :] = _rope(kv_ref[sl, 0:LANES], cdk_ref[sl, :], sdk_ref[sl, :], dhalf, LANES).astype(BF16)
            vc_scr[sl, :] = kv_ref[sl, LANES:2 * LANES].astype(BF16)
            lane_t = lax.broadcasted_iota(jnp.int32, (tb, LANES), 1)
            ki = jnp.where(lane_t < IDX_DIM, kv_ref[sl, 2 * LANES:3 * LANES], 0.0)
            ki = _rope(ki, cik_ref[sl, :], sik_ref[sl, :], ihalf, IDX_DIM)
            ki_scr[sl, :] = (ki + pltpu.roll(ki, IDX_DIM, axis=1)).astype(BF16)
            return 0

        lax.fori_loop(0, seq // tb, prep, 0)

    q_all = jnp.concatenate(
        [_rope(qc_ref[:, h * LANES:(h + 1) * LANES].astype(F32), cdq_ref[...], sdq_ref[...], dhalf, LANES).astype(BF16)
         for h in range(DSA_HEADS)], axis=0)
    qi_parts = []
    for p in range(IDX_HEADS // 2):
        xp = _rope(qi_ref[:, p * LANES:(p + 1) * LANES].astype(F32), ciq_ref[...], siq_ref[...], ihalf, IDX_DIM)
        qi_parts.append(jnp.where(lane < IDX_DIM, xp, 0.0).astype(BF16))
        qi_parts.append(jnp.where(lane >= IDX_DIM, xp, 0.0).astype(BF16))
    qi_all = jnp.concatenate(qi_parts, axis=0)
    w = wq_ref[:, 2 * LANES:3 * LANES] * ((IDX_HEADS ** -0.5) * (IDX_DIM ** -0.5))
    w_all = jnp.concatenate([jnp.sum(jnp.where(lane == IDX_W_LANE + h, w, 0.0), axis=1, keepdims=True)
                             for h in range(IDX_HEADS)], axis=0)

    nch = i // (ck // bq) + 1
    qpos = i * bq + lax.broadcasted_iota(jnp.int32, (bq, 1), 0)
    kcol = lax.broadcasted_iota(jnp.int32, (1, ck), 1)

    def idx_step(j, carry):
        mn, mx = carry
        off = pl.multiple_of(j * ck, ck)
        r = lax.dot_general(qi_all, ki_scr[pl.ds(off, ck), :], _NT, preferred_element_type=F32)
        r = jnp.maximum(r, 0.0) * w_all
        sc = r[0:bq]
        for h in range(1, IDX_HEADS):
            sc = sc + r[h * bq:(h + 1) * bq]
        valid = (off + kcol) <= qpos
        idx_scr[j] = jnp.where(valid, sc, -jnp.inf)
        mn = jnp.minimum(mn, jnp.min(jnp.where(valid, sc, jnp.inf), axis=1, keepdims=True))
        mx = jnp.maximum(mx, jnp.max(jnp.where(valid, sc, -jnp.inf), axis=1, keepdims=True))
        return mn, mx

    mn, mx = lax.fori_loop(0, nch, idx_step,
                           (jnp.full((bq, 1), jnp.inf, F32), jnp.full((bq, 1), -jnp.inf, F32)))

    def count_ge(t):
        def body(j, c):
            return c + jnp.sum(jnp.where(idx_scr[j] >= t, 1.0, 0.0), axis=1, keepdims=True)
        return lax.fori_loop(0, nch, body, jnp.zeros((bq, 1), F32))

    kf = float(topk)
    nvalid = (qpos + 1).astype(F32)
    c_mx = count_ge(mx)
    few = nvalid <= kf
    top_tie = (~few) & (c_mx >= kf)
    lo = jnp.where(top_tie, mx, mn)
    c_lo = jnp.where(top_tie, c_mx, nvalid)
    hi = jnp.where(top_tie, jnp.inf, mx)
    c_hi = jnp.where(top_tie, 0.0, c_mx)
    done = jnp.where(few | top_tie | (c_lo == kf), 1.0, 0.0)

    def bis_cond(st):
        it, _, _, _, _, done = st
        return jnp.logical_and(it < DSA_MAX_ITERS, jnp.min(done) < 0.5)

    def bis_body(st):
        it, lo, hi, c_lo, c_hi, done = st
        mid = 0.5 * lo + 0.5 * hi
        adjacent = (mid <= lo) | (mid >= hi)
        c = count_ge(mid)
        ge = c >= kf
        upd = (done < 0.5) & (~adjacent)
        lo = jnp.where(upd & ge, mid, lo)
        c_lo = jnp.where(upd & ge, c, c_lo)
        hi = jnp.where(upd & (~ge), mid, hi)
        c_hi = jnp.where(upd & (~ge), c, c_hi)
        done = jnp.where(adjacent | (c_lo == kf), 1.0, done)
        return it + 1, lo, hi, c_lo, c_hi, done

    _, lo, hi, c_lo, c_hi, _ = lax.while_loop(bis_cond, bis_body, (jnp.int32(0), lo, hi, c_lo, c_hi, done))

    need = kf - c_hi
    urow = lax.broadcasted_iota(jnp.int32, (ck, ck), 0)
    ucol = lax.broadcasted_iota(jnp.int32, (ck, ck), 1)
    upper = jnp.where(urow < ucol, 1.0, 0.0).astype(BF16)

    def sel_step(j, before):
        x = idx_scr[j]
        above = x >= hi
        eq = jnp.where((x >= lo) & (~above), 1.0, 0.0)
        prefix = jnp.dot(eq.astype(BF16), upper, preferred_element_type=F32) + before
        sel = above | ((eq > 0.5) & (prefix < need))
        idx_scr[j] = jnp.where(sel, 0.0, MASKED)
        return before + jnp.sum(eq, axis=1, keepdims=True)

    lax.fori_loop(0, nch, sel_step, jnp.zeros((bq, 1), F32))

    scale = DSA_DIM ** -0.5

    def att_step(j, carry):
        m, l, acc = carry
        off = pl.multiple_of(j * ck, ck)
        s = lax.dot_general(q_all, kc_scr[pl.ds(off, ck), :], _NT, preferred_element_type=F32) * scale
        s = (s.reshape(DSA_HEADS, bq, ck) + idx_scr[j][None]).reshape(rows, ck)
        m_new = jnp.maximum(m, jnp.max(s, axis=1, keepdims=True))
        alpha = jnp.exp(m - m_new)
        p = jnp.exp(s - m_new)
        l = alpha * l + jnp.sum(p, axis=1, keepdims=True)
        acc = alpha * acc + jnp.dot(p.astype(BF16), vc_scr[pl.ds(off, ck), :], preferred_element_type=F32)
        return m_new, l, acc

    _, l, acc = lax.fori_loop(0, nch, att_step, _init_carry(rows, DSA_DIM))
    o = acc / l
    for h in range(DSA_HEADS):
        o_ref[:, h * LANES:(h + 1) * LANES] = o[h * bq:(h + 1) * bq].astype(o_ref.dtype)


def _dsa_attention(proj, small, dsa_tabs, idx_tabs, batch, seq):
    t = batch * seq
    bq = DSA_BQ
    nq = seq // bq
    topk = min(TOPK_MAX, seq // 4)
    cd, sd = dsa_tabs
    ci, si = idx_tabs
    qrow = lambda b, i: (i, 0)
    full = lambda b, i: (0, 0)
    return pl.pallas_call(
        functools.partial(_dsa_kernel, seq=seq, topk=topk),
        grid=(batch, nq),
        in_specs=[pl.BlockSpec((bq, DSA_WIDTH), lambda b, i: (b * nq + i, 4096 // DSA_WIDTH)),
                  pl.BlockSpec((bq, IDX_HEADS * IDX_DIM), lambda b, i: (b * nq + i, 12288 // (IDX_HEADS * IDX_DIM))),
                  pl.BlockSpec((bq, 3 * LANES), lambda b, i: (b * nq + i, 0)),
                  pl.BlockSpec((seq, 3 * LANES), lambda b, i: (b, 0)),
                  pl.BlockSpec((bq, LANES), qrow), pl.BlockSpec((bq, LANES), qrow),
                  pl.BlockSpec((seq, LANES), full), pl.BlockSpec((seq, LANES), full),
                  pl.BlockSpec((bq, LANES), qrow), pl.BlockSpec((bq, LANES), qrow),
                  pl.BlockSpec((seq, LANES), full), pl.BlockSpec((seq, LANES), full)],
        out_specs=pl.BlockSpec((bq, DSA_WIDTH), lambda b, i: (b * nq + i, 0)),
        out_shape=jax.ShapeDtypeStruct((t, DSA_WIDTH), BF16),
        scratch_shapes=[pltpu.VMEM((seq, LANES), BF16), pltpu.VMEM((seq, LANES), BF16),
                        pltpu.VMEM((seq, LANES), BF16), pltpu.VMEM((seq // DSA_CK, bq, DSA_CK), F32)],
        compiler_params=_params("arbitrary", "arbitrary"),
        name="dsa_attention",
    )(proj, proj, small, small, cd, sd, cd, sd, ci, si, ci, si)


def _split(w, sizes):
    out, acc = [], 0
    for s in sizes:
        out.append(w[:, acc:acc + s])
        acc += s
    return out


def _even_layer(x2, batch, seq, norm_g, w_in, q_norm_g, w_uq, kv_norm_g, w_ukv, diff_lambda, subln_g, w_out,
                lambda_init, bq):
    d = x2.shape[1]
    c_q, c_kv, k_r, gate_a, dq, dk, dv, gate_b = _split(
        w_in, (MLA_Q_RANK, MLA_KV_RANK, MLA_ROPE, MLA_WIDTH, DIFF_WIDTH, DIFF_WIDTH, DIFF_WIDTH, DIFF_WIDTH))
    w_big = jnp.concatenate([gate_a, gate_b, dq, dk, dv], axis=1).astype(BF16)
    w_lat = jnp.concatenate([c_q, c_kv, k_r, jnp.zeros((d, LANES - MLA_ROPE), w_in.dtype)], axis=1).astype(BF16)
    uq = w_uq.reshape(MLA_Q_RANK, MLA_HEADS, MLA_NOPE + MLA_ROPE)
    uq = jnp.concatenate([uq, jnp.zeros((MLA_Q_RANK, MLA_HEADS, 2 * LANES - MLA_NOPE - MLA_ROPE), w_uq.dtype)], axis=2)
    uq = uq.reshape(MLA_Q_RANK, MLA_HEADS * 2 * LANES).astype(BF16)
    ukv = w_ukv.reshape(MLA_KV_RANK, MLA_HEADS, MLA_NOPE + MLA_V)
    ukv = jnp.concatenate([ukv[:, :, :MLA_NOPE].reshape(MLA_KV_RANK, -1), ukv[:, :, MLA_NOPE:].reshape(MLA_KV_RANK, -1)],
                          axis=1).astype(BF16)

    h = _rmsnorm(x2, norm_g, BF16)
    proj = _matmul(h, w_big, BF16, tm=1024, tn=1024, tk=512, name="even_inproj")
    lat = _matmul(h, w_lat, BF16, tm=1024, tn=w_lat.shape[1], tk=512, name="even_latent")
    q = _norm_matmul(lat, 0, q_norm_g, uq, tm=1024, tn=1024, name="mla_q_up")
    kv = _norm_matmul(lat, MLA_Q_RANK // MLA_KV_RANK, kv_norm_g, ukv, tm=1024, tn=1024, name="mla_kv_up")

    o_a = _mla_attention(q, kv, lat, _rope_tables(seq, MLA_ROPE, LANES), batch, seq, bq=bq)
    o_b = _diff_attention(proj, diff_lambda, subln_g, _rope_tables(seq, DIFF_DIM // ROPE_FRACTION, LANES),
                          batch, seq, lambda_init, bq=bq)
    o = jnp.concatenate([o_a, o_b], axis=1)
    return _outproj(o, proj, w_out.astype(BF16), x2, tm=1024, tn=1024, tk=512)


def _odd_layer(x2, batch, seq, norm_g, w_in, forget_bias, w_out, bq):
    d = x2.shape[1]
    (dsa_q, dsa_k, dsa_v, idx_q, idx_k, idx_w, gate_c, fox_q, fox_k, fox_v, fox_f, gate_d) = _split(
        w_in, (DSA_WIDTH, DSA_DIM, DSA_DIM, IDX_HEADS * IDX_DIM, IDX_DIM, IDX_HEADS, DSA_WIDTH,
               FOX_WIDTH, FOX_WIDTH, FOX_WIDTH, FOX_HEADS, FOX_WIDTH))
    w_big = jnp.concatenate([gate_c, gate_d, dsa_q, fox_q, fox_k, fox_v, idx_q], axis=1).astype(BF16)
    pad = 3 * LANES - (2 * DSA_DIM + IDX_DIM + IDX_HEADS + FOX_HEADS)
    w_small = jnp.concatenate([dsa_k, dsa_v, idx_k, idx_w, fox_f, jnp.zeros((d, pad), w_in.dtype)], axis=1).astype(BF16)

    h = _rmsnorm(x2, norm_g, BF16)
    proj = _matmul(h, w_big, BF16, tm=1024, tn=1024, tk=512, name="odd_inproj")
    small = _matmul(h, w_small, F32, tm=1024, tn=3 * LANES, tk=512, name="odd_small")

    o_c = _dsa_attention(proj, small, _rope_tables(seq, DSA_DIM // ROPE_FRACTION, LANES),
                         _rope_tables(seq, IDX_DIM // ROPE_FRACTION, IDX_DIM), batch, seq)
    cum = _logf_cumsum(small, forget_bias, batch, seq)
    nq = seq // bq
    cum_t = cum[:, FOX_F_LANE:FOX_F_LANE + FOX_HEADS].reshape(batch, seq, FOX_HEADS).transpose(0, 2, 1)
    cum_t = cum_t.reshape(batch * FOX_HEADS, nq, bq)
    o_d = _fox_attention(proj, cum, cum_t, batch, seq, bq=bq)
    o = jnp.concatenate([o_c, o_d], axis=1)
    return _outproj(o, proj, w_out.astype(BF16), x2, tm=1024, tn=1024, tk=512)


def kernel(x, even_norm, even_w_in, mla_q_norm, mla_w_uq, mla_kv_norm, mla_w_ukv, diff_lambda, diff_subln, even_w_out, odd_norm, odd_w_in, fox_forget_bias, odd_w_out, final_norm):
    batch, seq, d = x.shape
    bq = min(512, seq)
    h = x.reshape(batch * seq, d)
    depth = even_norm.shape[0] + odd_norm.shape[0]
    for layer in range(depth):
        i = layer // 2
        if layer % 2 == 0:
            lambda_init = 0.8 - 0.6 * math.exp(-0.3 * layer)
            h = _even_layer(h, batch, seq, even_norm[i], even_w_in[i], mla_q_norm[i], mla_w_uq[i], mla_kv_norm[i],
                            mla_w_ukv[i], diff_lambda[i], diff_subln[i], even_w_out[i], lambda_init, bq)
        else:
            h = _odd_layer(h, batch, seq, odd_norm[i], odd_w_in[i], fox_forget_bias[i], odd_w_out[i], bq)
    return _rmsnorm(h, final_norm, x.dtype).reshape(batch, seq, d)
```

```python
import functools
import math

import jax
import jax.numpy as jnp
from jax import lax
from jax.experimental import pallas as pl
from jax.experimental.pallas import tpu as pltpu

F32 = jnp.float32
BF16 = jnp.bfloat16

ROPE_THETA = 500000.0
NORM_EPS = 1e-6
ROPE_FRACTION = 4

MLA_HEADS, MLA_NOPE, MLA_ROPE, MLA_V = 16, 128, 64, 128
MLA_Q_RANK, MLA_KV_RANK = 1024, 512
DIFF_HEADS, DIFF_DIM = 8, 128
DSA_HEADS, DSA_DIM = 16, 128
IDX_HEADS, IDX_DIM = 16, 64
TOPK_MAX = 256
FOX_HEADS, FOX_DIM = 16, 128

MLA_WIDTH = MLA_HEADS * MLA_V
DIFF_WIDTH = DIFF_HEADS * 2 * DIFF_DIM
DSA_WIDTH = DSA_HEADS * DSA_DIM
FOX_WIDTH = FOX_HEADS * FOX_DIM
GATE_WIDTH = MLA_WIDTH + DIFF_WIDTH

LANES = 128
VMEM_LIMIT_BYTES = 56 * 1024 * 1024
MASKED = -1e30

IDX_W_LANE = IDX_DIM
FOX_F_LANE = IDX_DIM + IDX_HEADS

_NT = (((1,), (1,)), ((), ()))


def _params(*sem):
    return pltpu.CompilerParams(dimension_semantics=sem, vmem_limit_bytes=VMEM_LIMIT_BYTES)


def _rmsnorm_kernel(x_ref, g_ref, o_ref):
    x = x_ref[...].astype(F32)
    y = x * lax.rsqrt(jnp.mean(x * x, axis=-1, keepdims=True) + NORM_EPS)
    o_ref[...] = (y * g_ref[...]).astype(o_ref.dtype)


def _rmsnorm(x, g, out_dtype, tm=256):
    t, d = x.shape
    return pl.pallas_call(
        _rmsnorm_kernel,
        grid=(t // tm,),
        in_specs=[pl.BlockSpec((tm, d), lambda i: (i, 0)), pl.BlockSpec((1, d), lambda i: (0, 0))],
        out_specs=pl.BlockSpec((tm, d), lambda i: (i, 0)),
        out_shape=jax.ShapeDtypeStruct((t, d), out_dtype),
        compiler_params=_params("arbitrary"),
        name="rmsnorm",
    )(x, g.reshape(1, d).astype(F32))


def _mm_kernel(a_ref, b_ref, o_ref):
    o_ref[...] = jnp.dot(a_ref[...], b_ref[...], preferred_element_type=F32).astype(o_ref.dtype)


def _matmul(a, b, out_dtype, *, tm, tn, name):
    m, kdim = a.shape
    n = b.shape[1]
    return pl.pallas_call(
        _mm_kernel,
        grid=(m // tm, n // tn),
        in_specs=[pl.BlockSpec((tm, kdim), lambda i, j: (i, 0)),
                  pl.BlockSpec((kdim, tn), lambda i, j: (0, j))],
        out_specs=pl.BlockSpec((tm, tn), lambda i, j: (i, j)),
        out_shape=jax.ShapeDtypeStruct((m, n), out_dtype),
        compiler_params=_params("arbitrary", "arbitrary"),
        name=name,
    )(a, b)


def _norm_mm_kernel(a_ref, g_ref, b_ref, o_ref):
    x = a_ref[...].astype(F32)
    y = x * lax.rsqrt(jnp.mean(x * x, axis=-1, keepdims=True) + NORM_EPS) * g_ref[...]
    o_ref[...] = jnp.dot(y.astype(BF16), b_ref[...], preferred_element_type=F32).astype(o_ref.dtype)


def _norm_matmul(a, a_col_block, g, b, *, tm, tn, name):
    m = a.shape[0]
    kdim, n = b.shape
    return pl.pallas_call(
        _norm_mm_kernel,
        grid=(m // tm, n // tn),
        in_specs=[pl.BlockSpec((tm, kdim), lambda i, j: (i, a_col_block)),
                  pl.BlockSpec((1, kdim), lambda i, j: (0, 0)),
                  pl.BlockSpec((kdim, tn), lambda i, j: (0, j))],
        out_specs=pl.BlockSpec((tm, tn), lambda i, j: (i, j)),
        out_shape=jax.ShapeDtypeStruct((m, n), BF16),
        compiler_params=_params("arbitrary", "arbitrary"),
        name=name,
    )(a, g.reshape(1, kdim).astype(F32), b)


def _outproj_kernel(oa_ref, ob_ref, w_ref, res_ref, h_ref):
    ka = oa_ref.shape[1]
    acc = jnp.dot(oa_ref[...], w_ref[:ka, :], preferred_element_type=F32)
    acc = acc + jnp.dot(ob_ref[...], w_ref[ka:, :], preferred_element_type=F32)
    h_ref[...] = acc + res_ref[...]


def _outproj(oa, ob, w, res, *, tm, tn):
    m, ka = oa.shape
    kb = ob.shape[1]
    n = w.shape[1]
    return pl.pallas_call(
        _outproj_kernel,
        grid=(m // tm, n // tn),
        in_specs=[pl.BlockSpec((tm, ka), lambda i, j: (i, 0)),
                  pl.BlockSpec((tm, kb), lambda i, j: (i, 0)),
                  pl.BlockSpec((ka + kb, tn), lambda i, j: (0, j)),
                  pl.BlockSpec((tm, tn), lambda i, j: (i, j))],
        out_specs=pl.BlockSpec((tm, tn), lambda i, j: (i, j)),
        out_shape=jax.ShapeDtypeStruct((m, n), F32),
        compiler_params=_params("arbitrary", "arbitrary"),
        name="outproj",
    )(oa, ob, w, res)


def _rope_tables(seq, rot_dim, period, scale=1.0):
    inv = ROPE_THETA ** (-jnp.arange(0, rot_dim, 2, dtype=F32) / rot_dim)
    ang = jnp.arange(seq, dtype=F32)[:, None] * inv[None, :]
    cos, sin = jnp.cos(ang), jnp.sin(ang)
    ones = jnp.ones((seq, period - rot_dim), F32)
    c = jnp.concatenate([cos, cos, ones], axis=1) * scale
    s = jnp.concatenate([-sin, sin, 0.0 * ones], axis=1) * scale
    reps = LANES // period
    return jnp.tile(c, (1, reps)), jnp.tile(s, (1, reps))


def _rope(x, c, s, half, period):
    lane = lax.broadcasted_iota(jnp.int32, x.shape, 1)
    first = (lane & (period - 1)) < half
    partner = jnp.where(first, pltpu.roll(x, LANES - half, axis=1), pltpu.roll(x, half, axis=1))
    return x * c + partner * s


def _softmax_pv(s, v, carry, mask=None):
    m, l, acc = carry
    if mask is not None:
        s = jnp.where(mask, s, MASKED)
    m_new = jnp.maximum(m, jnp.max(s, axis=1, keepdims=True))
    alpha = jnp.exp(m - m_new)
    p = jnp.exp(s - m_new)
    l = alpha * l + jnp.sum(p, axis=1, keepdims=True)
    acc = alpha * acc + jnp.dot(p.astype(BF16), v, preferred_element_type=F32)
    return m_new, l, acc


def _init_carry(rows, dv):
    return (jnp.full((rows, 1), MASKED, F32), jnp.zeros((rows, 1), F32), jnp.zeros((rows, dv), F32))


def _diag_mask(bq):
    row = lax.broadcasted_iota(jnp.int32, (bq, bq), 0)
    col = lax.broadcasted_iota(jnp.int32, (bq, bq), 1)
    return row >= col


def _causal_sweep(logits, update, i, state):
    state = lax.fori_loop(0, i, lambda j, st: update(j, logits(j), st, False), state)
    return update(i, logits(i), state, True)


def _silu(g):
    return g * jax.nn.sigmoid(g)


def _mla_kernel(q_ref, kn_ref, kr_ref, v_ref, gate_ref, cq_ref, sq_ref, ck_ref, sk_ref, o_ref, k_scr, *, bq, scale):
    i = pl.program_id(2)
    half = MLA_ROPE // 2

    @pl.when(i == 0)
    def _():
        k_scr[:, :LANES] = kn_ref[...]
        k_scr[:, LANES:] = _rope(kr_ref[...].astype(F32), ck_ref[...], sk_ref[...], half, LANES).astype(BF16)

    qn = (q_ref[:, :LANES].astype(F32) * scale).astype(BF16)
    qr = _rope(q_ref[:, LANES:].astype(F32), cq_ref[...], sq_ref[...], half, LANES).astype(BF16)
    q = jnp.concatenate([qn, qr], axis=1)
    mask = _diag_mask(bq)

    def logits(j):
        off = pl.multiple_of(j * bq, bq)
        return lax.dot_general(q, k_scr[pl.ds(off, bq), :], _NT, preferred_element_type=F32)

    def update(j, s, st, diag):
        off = pl.multiple_of(j * bq, bq)
        return _softmax_pv(s, v_ref[pl.ds(off, bq), :], st, mask if diag else None)

    _, l, acc = _causal_sweep(logits, update, i, _init_carry(bq, MLA_V))
    o_ref[...] = (acc / l * _silu(gate_ref[...].astype(F32))).astype(o_ref.dtype)


def _mla_attention(q, kv, lat, proj, tabs_q, tabs_k, batch, seq, *, bq):
    t = batch * seq
    nq = seq // bq
    kr_block = (MLA_Q_RANK + MLA_KV_RANK) // LANES
    qrow = lambda b, h, i: (i, 0)
    full = lambda b, h, i: (0, 0)
    return pl.pallas_call(
        functools.partial(_mla_kernel, bq=bq, scale=(MLA_NOPE + MLA_ROPE) ** -0.5),
        grid=(batch, MLA_HEADS, nq),
        in_specs=[pl.BlockSpec((bq, 2 * LANES), lambda b, h, i: (b * nq + i, h)),
                  pl.BlockSpec((seq, LANES), lambda b, h, i: (b, h)),
                  pl.BlockSpec((seq, LANES), lambda b, h, i: (b, kr_block)),
                  pl.BlockSpec((seq, LANES), lambda b, h, i: (b, MLA_HEADS + h)),
                  pl.BlockSpec((bq, MLA_V), lambda b, h, i: (b * nq + i, h)),
                  pl.BlockSpec((bq, LANES), qrow), pl.BlockSpec((bq, LANES), qrow),
                  pl.BlockSpec((seq, LANES), full), pl.BlockSpec((seq, LANES), full)],
        out_specs=pl.BlockSpec((bq, MLA_V), lambda b, h, i: (b * nq + i, h)),
        out_shape=jax.ShapeDtypeStruct((t, MLA_WIDTH), BF16),
        scratch_shapes=[pltpu.VMEM((seq, 2 * LANES), BF16)],
        compiler_params=_params("arbitrary", "arbitrary", "arbitrary"),
        name="mla_attention",
    )(q, kv, lat, kv, proj, *tabs_q, *tabs_k)


def _diff_kernel(q_ref, k_ref, v_ref, gate_ref, lam_ref, g_ref, cq_ref, sq_ref, ck_ref, sk_ref, o_ref, k_scr,
                 *, bq, lambda_init):
    i = pl.program_id(2)
    half = DIFF_DIM // ROPE_FRACTION // 2

    @pl.when(i == 0)
    def _():
        for c in range(2):
            sl = slice(c * LANES, (c + 1) * LANES)
            k_scr[:, sl] = _rope(k_ref[:, sl].astype(F32), ck_ref[...], sk_ref[...], half, LANES).astype(BF16)

    qs = [_rope(q_ref[:, c * LANES:(c + 1) * LANES].astype(F32), cq_ref[...], sq_ref[...], half, LANES).astype(BF16)
          for c in range(2)]
    mask = _diag_mask(bq)

    def logits(j):
        off = pl.multiple_of(j * bq, bq)
        return tuple(lax.dot_general(qs[c], k_scr[pl.ds(off, bq), c * LANES:(c + 1) * LANES], _NT,
                                     preferred_element_type=F32) for c in range(2))

    def update(j, s, st, diag):
        v = v_ref[pl.ds(pl.multiple_of(j * bq, bq), bq), :]
        return tuple(_softmax_pv(s[c], v, st[c], mask if diag else None) for c in range(2))

    init = (_init_carry(bq, 2 * DIFF_DIM), _init_carry(bq, 2 * DIFF_DIM))
    (_, l0, acc0), (_, l1, acc1) = _causal_sweep(logits, update, i, init)

    lp = lam_ref[...]
    lam = (jnp.exp(jnp.sum(lp[0:1] * lp[1:2], axis=1, keepdims=True))
           - jnp.exp(jnp.sum(lp[2:3] * lp[3:4], axis=1, keepdims=True)) + lambda_init)
    o = acc0 / l0 - lam * (acc1 / l1)
    o = o * lax.rsqrt(jnp.mean(o * o, axis=-1, keepdims=True) + NORM_EPS) * g_ref[...]
    o_ref[...] = (o * (1.0 - lambda_init) * _silu(gate_ref[...].astype(F32))).astype(o_ref.dtype)


def _diff_attention(proj, lam_params, subln_g, tabs_q, tabs_k, batch, seq, lambda_init, *, bq):
    t = batch * seq
    nq = seq // bq
    w = 2 * DIFF_DIM
    g0, q0, k0, v0 = MLA_WIDTH // w, GATE_WIDTH // w, (GATE_WIDTH + DIFF_WIDTH) // w, (GATE_WIDTH + 2 * DIFF_WIDTH) // w
    qrow = lambda b, h, i: (i, 0)
    full = lambda b, h, i: (0, 0)
    return pl.pallas_call(
        functools.partial(_diff_kernel, bq=bq, lambda_init=lambda_init),
        grid=(batch, DIFF_HEADS, nq),
        in_specs=[pl.BlockSpec((bq, w), lambda b, h, i: (b * nq + i, q0 + h)),
                  pl.BlockSpec((seq, w), lambda b, h, i: (b, k0 + h)),
                  pl.BlockSpec((seq, w), lambda b, h, i: (b, v0 + h)),
                  pl.BlockSpec((bq, w), lambda b, h, i: (b * nq + i, g0 + h)),
                  pl.BlockSpec((4, DIFF_DIM), full),
                  pl.BlockSpec((1, w), full),
                  pl.BlockSpec((bq, LANES), qrow), pl.BlockSpec((bq, LANES), qrow),
                  pl.BlockSpec((seq, LANES), full), pl.BlockSpec((seq, LANES), full)],
        out_specs=pl.BlockSpec((bq, w), lambda b, h, i: (b * nq + i, h)),
        out_shape=jax.ShapeDtypeStruct((t, DIFF_WIDTH), BF16),
        scratch_shapes=[pltpu.VMEM((seq, w), BF16)],
        compiler_params=_params("arbitrary", "arbitrary", "arbitrary"),
        name="diff_attention",
    )(proj, proj, proj, proj, lam_params.astype(F32), subln_g.reshape(1, w).astype(F32), *tabs_q, *tabs_k)


def _fox_kernel(q_ref, k_ref, v_ref, gate_ref, cum_ref, cumt_ref, o_ref, *, bq, scale):
    h = pl.program_id(1)
    i = pl.program_id(2)
    q = (q_ref[...].astype(F32) * scale).astype(BF16)
    cum = cum_ref[...]
    lane = lax.broadcasted_iota(jnp.int32, cum.shape, 1)
    cq = jnp.sum(jnp.where(lane == FOX_F_LANE + h, cum, 0.0), axis=1, keepdims=True)
    mask = _diag_mask(bq)

    def logits(j):
        off = pl.multiple_of(j * bq, bq)
        s = lax.dot_general(q, k_ref[pl.ds(off, bq), :], _NT, preferred_element_type=F32)
        return s + (cq - cumt_ref[0, pl.ds(j, 1), :])

    def update(j, s, st, diag):
        off = pl.multiple_of(j * bq, bq)
        return _softmax_pv(s, v_ref[pl.ds(off, bq), :], st, mask if diag else None)

    _, l, acc = _causal_sweep(logits, update, i, _init_carry(bq, FOX_DIM))
    o_ref[...] = (acc / l * _silu(gate_ref[...].astype(F32))).astype(o_ref.dtype)


def _fox_attention(proj, cum, cum_t, batch, seq, *, bq):
    t = batch * seq
    nq = seq // bq
    g0 = DSA_WIDTH // LANES
    q0 = (GATE_WIDTH + DSA_WIDTH) // LANES
    k0, v0 = q0 + FOX_HEADS, q0 + 2 * FOX_HEADS
    return pl.pallas_call(
        functools.partial(_fox_kernel, bq=bq, scale=FOX_DIM ** -0.5),
        grid=(batch, FOX_HEADS, nq),
        in_specs=[pl.BlockSpec((bq, LANES), lambda b, h, i: (b * nq + i, q0 + h)),
                  pl.BlockSpec((seq, LANES), lambda b, h, i: (b, k0 + h)),
                  pl.BlockSpec((seq, LANES), lambda b, h, i: (b, v0 + h)),
                  pl.BlockSpec((bq, LANES), lambda b, h, i: (b * nq + i, g0 + h)),
                  pl.BlockSpec((bq, LANES), lambda b, h, i: (b * nq + i, 0)),
                  pl.BlockSpec((1, nq, bq), lambda b, h, i: (b * FOX_HEADS + h, 0, 0))],
        out_specs=pl.BlockSpec((bq, FOX_DIM), lambda b, h, i: (b * nq + i, h)),
        out_shape=jax.ShapeDtypeStruct((t, FOX_WIDTH), BF16),
        compiler_params=_params("arbitrary", "arbitrary", "arbitrary"),
        name="fox_attention",
    )(proj, proj, proj, proj, cum, cum_t)


def _logf_cumsum_kernel(f_ref, b_ref, o_ref, carry_ref, *, tb):
    @pl.when(pl.program_id(1) == 0)
    def _():
        carry_ref[...] = jnp.zeros_like(carry_ref)

    x = f_ref[...] + b_ref[...]
    logf = jnp.minimum(x, 0.0) - jnp.log1p(jnp.exp(-jnp.abs(x)))
    row = lax.broadcasted_iota(jnp.int32, (tb, tb), 0)
    col = lax.broadcasted_iota(jnp.int32, (tb, tb), 1)
    tri = jnp.where(row >= col, 1.0, 0.0).astype(BF16)
    hi = logf.astype(BF16)
    r1 = logf - hi.astype(F32)
    mid = r1.astype(BF16)
    lo = (r1 - mid.astype(F32)).astype(BF16)
    cum = (jnp.dot(tri, hi, preferred_element_type=F32) + jnp.dot(tri, mid, preferred_element_type=F32)
           + jnp.dot(tri, lo, preferred_element_type=F32)) + carry_ref[...]
    o_ref[...] = cum
    carry_ref[...] = cum[tb - 1:tb, :]


def _logf_cumsum(small, forget_bias, batch, seq, *, tb=512):
    t = batch * seq
    nb = seq // tb
    bias = jnp.zeros((1, LANES), F32).at[0, FOX_F_LANE:FOX_F_LANE + FOX_HEADS].set(forget_bias.astype(F32))
    return pl.pallas_call(
        functools.partial(_logf_cumsum_kernel, tb=tb),
        grid=(batch, nb),
        in_specs=[pl.BlockSpec((tb, LANES), lambda b, i: (b * nb + i, 2)),
                  pl.BlockSpec((1, LANES), lambda b, i: (0, 0))],
        out_specs=pl.BlockSpec((tb, LANES), lambda b, i: (b * nb + i, 0)),
        out_shape=jax.ShapeDtypeStruct((t, LANES), F32),
        scratch_shapes=[pltpu.VMEM((1, LANES), F32)],
        compiler_params=_params("arbitrary", "arbitrary"),
        name="logf_cumsum",
    )(small, bias)


DSA_CK = 512
DSA_BQ = 128
DSA_GROUP = 4
DSA_MAX_ITERS = 640


def _row_total(x):
    return jnp.broadcast_to(jnp.sum(x, axis=1, keepdims=True), x.shape)


def _tile_lanes(x, n):
    return jnp.concatenate([x] * n, axis=1)


def _dsa_kernel(qc_ref, qi_ref, wq_ref, kv_ref, gate_ref, cdq_ref, sdq_ref, cdk_ref, sdk_ref, ciq_ref, siq_ref,
                cik_ref, sik_ref, o_ref, kc_scr, vc_scr, ki_scr, idx_scr, q_scr, qi_scr, w_scr, *, seq, topk):
    i = pl.program_id(1)
    bq, ck = DSA_BQ, DSA_CK
    dhalf = DSA_DIM // ROPE_FRACTION // 2
    ihalf = IDX_DIM // ROPE_FRACTION // 2
    rows = DSA_HEADS * bq
    lane = lax.broadcasted_iota(jnp.int32, (bq, LANES), 1)

    @pl.when(i == 0)
    def _():
        tb = 512

        def prep(r, _):
            off = pl.multiple_of(r * tb, tb)
            sl = pl.ds(off, tb)
            kc_scr[sl, :] = _rope(kv_ref[sl, 0:LANES], cdk_ref[sl, :], sdk_ref[sl, :], dhalf, LANES).astype(BF16)
            vc_scr[sl, :LANES] = kv_ref[sl, LANES:2 * LANES].astype(BF16)
            vc_scr[sl, LANES:] = jnp.ones((tb, LANES), BF16)
            lane_t = lax.broadcasted_iota(jnp.int32, (tb, LANES), 1)
            ki = jnp.where(lane_t < IDX_DIM, kv_ref[sl, 2 * LANES:3 * LANES], 0.0)
            ki = _rope(ki, cik_ref[sl, :], sik_ref[sl, :], ihalf, IDX_DIM)
            ki_scr[sl, :] = (ki + pltpu.roll(ki, IDX_DIM, axis=1)).astype(BF16)
            return 0

        lax.fori_loop(0, seq // tb, prep, 0)

    for h in range(DSA_HEADS):
        xh = qc_ref[:, h * LANES:(h + 1) * LANES].astype(F32)
        q_scr[h * bq:(h + 1) * bq, :] = _rope(xh, cdq_ref[...], sdq_ref[...], dhalf, LANES).astype(BF16)
    for p in range(IDX_HEADS // 2):
        xp = _rope(qi_ref[:, p * LANES:(p + 1) * LANES].astype(F32), ciq_ref[...], siq_ref[...], ihalf, IDX_DIM)
        qi_scr[(2 * p) * bq:(2 * p + 1) * bq, :] = jnp.where(lane < IDX_DIM, xp, 0.0).astype(BF16)
        qi_scr[(2 * p + 1) * bq:(2 * p + 2) * bq, :] = jnp.where(lane >= IDX_DIM, xp, 0.0).astype(BF16)
    w = wq_ref[:, 2 * LANES:3 * LANES] * ((IDX_HEADS ** -0.5) * (IDX_DIM ** -0.5))
    for h in range(IDX_HEADS):
        w_scr[h * bq:(h + 1) * bq, :] = _row_total(jnp.where(lane == IDX_W_LANE + h, w, 0.0))

    nch = i // (ck // bq) + 1
    qpos = i * bq + lax.broadcasted_iota(jnp.int32, (bq, LANES), 0)
    inf = jnp.full((bq, LANES), jnp.inf, F32)

    def idx_step(j, carry):
        mn, mx = carry
        off = pl.multiple_of(j * ck, ck)
        r = lax.dot_general(qi_scr[...], ki_scr[pl.ds(off, ck), :], _NT, preferred_element_type=F32)
        halves = []
        for c in range(ck // LANES):
            sc = None
            for h in range(IDX_HEADS):
                term = jnp.maximum(r[h * bq:(h + 1) * bq, c * LANES:(c + 1) * LANES], 0.0) * w_scr[h * bq:(h + 1) * bq, :]
                sc = term if sc is None else sc + term
            valid = (off + c * LANES + lane) <= qpos
            halves.append(jnp.where(valid, sc, -inf))
            mn = jnp.minimum(mn, jnp.where(valid, sc, inf))
            mx = jnp.maximum(mx, jnp.where(valid, sc, -inf))
        idx_scr[j] = jnp.concatenate(halves, axis=1)
        return mn, mx

    mn, mx = lax.fori_loop(0, nch, idx_step, (inf, -inf))
    mn = jnp.broadcast_to(jnp.min(mn, axis=1, keepdims=True), mn.shape)
    mx = jnp.broadcast_to(jnp.max(mx, axis=1, keepdims=True), mx.shape)

    def count_ge(t):
        def body(j, c):
            x = idx_scr[j]
            for half in range(ck // LANES):
                c = c + jnp.where(x[:, half * LANES:(half + 1) * LANES] >= t, 1.0, 0.0)
            return c
        return _row_total(lax.fori_loop(0, nch, body, jnp.zeros((bq, LANES), F32)))

    kf = float(topk)
    nvalid = (qpos + 1).astype(F32)
    c_mx = count_ge(mx)
    few = nvalid <= kf
    top_tie = (~few) & (c_mx >= kf)
    lo = jnp.where(top_tie, mx, mn)
    c_lo = jnp.where(top_tie, c_mx, nvalid)
    hi = jnp.where(top_tie, inf, mx)
    c_hi = jnp.where(top_tie, 0.0, c_mx)
    done = jnp.where(few | top_tie | (c_lo == kf), 1.0, 0.0)

    def search_cond(st):
        it, _, _, _, _, done = st
        return jnp.logical_and(it < DSA_MAX_ITERS, jnp.min(done) < 0.5)

    def search_body(st):
        it, lo, hi, c_lo, c_hi, done = st
        mid = 0.5 * lo + 0.5 * hi
        adjacent = (mid <= lo) | (mid >= hi)
        guess = lo + (hi - lo) * ((c_lo - kf - 0.5) / jnp.maximum(c_lo - c_hi, 1.0))
        use_guess = jnp.logical_and((it & 1) == 1, (guess > lo) & (guess < hi))
        t = jnp.where(use_guess, guess, mid)
        c = count_ge(t)
        ge = c >= kf
        upd = (done < 0.5) & (~adjacent)
        lo = jnp.where(upd & ge, t, lo)
        c_lo = jnp.where(upd & ge, c, c_lo)
        hi = jnp.where(upd & (~ge), t, hi)
        c_hi = jnp.where(upd & (~ge), c, c_hi)
        done = jnp.where(adjacent | (c_lo == kf), 1.0, done)
        return it + 1, lo, hi, c_lo, c_hi, done

    _, lo, hi, c_lo, c_hi, _ = lax.while_loop(search_cond, search_body, (jnp.int32(1), lo, hi, c_lo, c_hi, done))

    nl = ck // LANES
    lo_w = _tile_lanes(lo, nl)

    def write_plain():
        def body(j, _):
            idx_scr[j] = jnp.where(idx_scr[j] >= lo_w, 0.0, MASKED)
            return 0
        lax.fori_loop(0, nch, body, 0)

    def write_ties():
        hi_w = _tile_lanes(hi, nl)
        need = _tile_lanes(kf - c_hi, nl)
        urow = lax.broadcasted_iota(jnp.int32, (ck, ck), 0)
        ucol = lax.broadcasted_iota(jnp.int32, (ck, ck), 1)
        upper = jnp.where(urow < ucol, 1.0, 0.0).astype(BF16)

        def body(j, before):
            x = idx_scr[j]
            above = x >= hi_w
            eq = jnp.where((x >= lo_w) & (~above), 1.0, 0.0)
            prefix = jnp.dot(eq.astype(BF16), upper, preferred_element_type=F32) + _tile_lanes(before, nl)
            sel = above | ((eq > 0.5) & (prefix < need))
            idx_scr[j] = jnp.where(sel, 0.0, MASKED)
            tot = eq[:, :LANES]
            for c in range(1, nl):
                tot = tot + eq[:, c * LANES:(c + 1) * LANES]
            return before + _row_total(tot)

        lax.fori_loop(0, nch, body, jnp.zeros((bq, LANES), F32))

    lax.cond(jnp.max(c_lo) > kf, write_ties, write_plain)

    grp = DSA_GROUP * bq
    ngrp = rows // grp

    def att_step(j, carry):
        off = pl.multiple_of(j * ck, ck)
        k = kc_scr[pl.ds(off, ck), :]
        v = vc_scr[pl.ds(off, ck), :]
        bias = jnp.concatenate([idx_scr[j]] * DSA_GROUP, axis=0)
        out = []
        for g in range(ngrp):
            m, acc = carry[g]
            s = lax.dot_general(q_scr[g * grp:(g + 1) * grp, :], k, _NT, preferred_element_type=F32)
            parts = [s[:, c * LANES:(c + 1) * LANES] + bias[:, c * LANES:(c + 1) * LANES] for c in range(nl)]
            mc = parts[0]
            for c in range(1, nl):
                mc = jnp.maximum(mc, parts[c])
            m_new = jnp.maximum(m, jnp.max(mc, axis=1, keepdims=True))
            alpha = jnp.exp(m - m_new)
            p = jnp.concatenate([jnp.exp(x - m_new) for x in parts], axis=1).astype(BF16)
            acc = _tile_lanes(alpha, 2) * acc + jnp.dot(p, v, preferred_element_type=F32)
            out.append((m_new, acc))
        return tuple(out)

    init = tuple((jnp.full((grp, LANES), MASKED, F32), jnp.zeros((grp, 2 * LANES), F32)) for _ in range(ngrp))
    res = lax.fori_loop(0, nch, att_step, init)
    for g in range(ngrp):
        _, acc = res[g]
        o = acc[:, :LANES] / acc[:, LANES:]
        for t in range(DSA_GROUP):
            hl = slice((g * DSA_GROUP + t) * LANES, (g * DSA_GROUP + t + 1) * LANES)
            o_ref[:, hl] = (o[t * bq:(t + 1) * bq] * _silu(gate_ref[:, hl].astype(F32))).astype(o_ref.dtype)


def _dsa_attention(proj, small, dsa_tabs_q, dsa_tabs_k, idx_tabs, batch, seq):
    t = batch * seq
    bq = DSA_BQ
    nq = seq // bq
    topk = min(TOPK_MAX, seq // 4)
    rows = DSA_HEADS * bq
    idx_w = IDX_HEADS * IDX_DIM
    qrow = lambda b, i: (i, 0)
    full = lambda b, i: (0, 0)
    return pl.pallas_call(
        functools.partial(_dsa_kernel, seq=seq, topk=topk),
        grid=(batch, nq),
        in_specs=[pl.BlockSpec((bq, DSA_WIDTH), lambda b, i: (b * nq + i, GATE_WIDTH // DSA_WIDTH)),
                  pl.BlockSpec((bq, idx_w), lambda b, i: (b * nq + i, (GATE_WIDTH + DSA_WIDTH + 3 * FOX_WIDTH) // idx_w)),
                  pl.BlockSpec((bq, 3 * LANES), lambda b, i: (b * nq + i, 0)),
                  pl.BlockSpec((seq, 3 * LANES), lambda b, i: (b, 0)),
                  pl.BlockSpec((bq, DSA_WIDTH), lambda b, i: (b * nq + i, 0)),
                  pl.BlockSpec((bq, LANES), qrow), pl.BlockSpec((bq, LANES), qrow),
                  pl.BlockSpec((seq, LANES), full), pl.BlockSpec((seq, LANES), full),
                  pl.BlockSpec((bq, LANES), qrow), pl.BlockSpec((bq, LANES), qrow),
                  pl.BlockSpec((seq, LANES), full), pl.BlockSpec((seq, LANES), full)],
        out_specs=pl.BlockSpec((bq, DSA_WIDTH), lambda b, i: (b * nq + i, 0)),
        out_shape=jax.ShapeDtypeStruct((t, DSA_WIDTH), BF16),
        scratch_shapes=[pltpu.VMEM((seq, LANES), BF16), pltpu.VMEM((seq, 2 * LANES), BF16),
                        pltpu.VMEM((seq, LANES), BF16), pltpu.VMEM((seq // DSA_CK, bq, DSA_CK), F32),
                        pltpu.VMEM((rows, LANES), BF16), pltpu.VMEM((rows, LANES), BF16),
                        pltpu.VMEM((rows, LANES), F32)],
        compiler_params=_params("arbitrary", "arbitrary"),
        name="dsa_attention",
    )(proj, proj, small, small, proj, *dsa_tabs_q, *dsa_tabs_k, *idx_tabs, *idx_tabs)


def _split(w, sizes):
    out, acc = [], 0
    for s in sizes:
        out.append(w[:, acc:acc + s])
        acc += s
    return out


def _bf16_cols(parts):
    return jnp.concatenate([p.astype(BF16) for p in parts], axis=1)


def _even_layer(x2, batch, seq, norm_g, w_in, q_norm_g, w_uq, kv_norm_g, w_ukv, diff_lambda, subln_g, w_out,
                lambda_init, bq):
    d = x2.shape[1]
    c_q, c_kv, k_r, gate_a, dq, dk, dv, gate_b = _split(
        w_in, (MLA_Q_RANK, MLA_KV_RANK, MLA_ROPE, MLA_WIDTH, DIFF_WIDTH, DIFF_WIDTH, DIFF_WIDTH, DIFF_WIDTH))
    w_big = _bf16_cols([gate_a, gate_b, dq, dk, dv])
    w_lat = _bf16_cols([c_q, c_kv, k_r, jnp.zeros((d, LANES - MLA_ROPE), w_in.dtype)])
    uq = w_uq.astype(BF16).reshape(MLA_Q_RANK, MLA_HEADS, MLA_NOPE + MLA_ROPE)
    uq = jnp.concatenate([uq, jnp.zeros((MLA_Q_RANK, MLA_HEADS, 2 * LANES - MLA_NOPE - MLA_ROPE), BF16)], axis=2)
    uq = uq.reshape(MLA_Q_RANK, MLA_HEADS * 2 * LANES)
    ukv = w_ukv.astype(BF16).reshape(MLA_KV_RANK, MLA_HEADS, MLA_NOPE + MLA_V)
    ukv = jnp.concatenate([ukv[:, :, :MLA_NOPE].reshape(MLA_KV_RANK, -1), ukv[:, :, MLA_NOPE:].reshape(MLA_KV_RANK, -1)],
                          axis=1)

    h = _rmsnorm(x2, norm_g, BF16)
    proj = _matmul(h, w_big, BF16, tm=1024, tn=1024, name="even_inproj")
    lat = _matmul(h, w_lat, BF16, tm=512, tn=w_lat.shape[1], name="even_latent")
    q = _norm_matmul(lat, 0, q_norm_g, uq, tm=1024, tn=1024, name="mla_q_up")
    kv = _norm_matmul(lat, MLA_Q_RANK // MLA_KV_RANK, kv_norm_g, ukv, tm=1024, tn=1024, name="mla_kv_up")

    mla_scale = (MLA_NOPE + MLA_ROPE) ** -0.5
    o_a = _mla_attention(q, kv, lat, proj, _rope_tables(seq, MLA_ROPE, LANES, mla_scale),
                         _rope_tables(seq, MLA_ROPE, LANES), batch, seq, bq=bq)
    rot = DIFF_DIM // ROPE_FRACTION
    o_b = _diff_attention(proj, diff_lambda, subln_g, _rope_tables(seq, rot, LANES, DIFF_DIM ** -0.5),
                          _rope_tables(seq, rot, LANES), batch, seq, lambda_init, bq=bq)
    return _outproj(o_a, o_b, w_out.astype(BF16), x2, tm=1024, tn=1024)


def _odd_layer(x2, batch, seq, norm_g, w_in, forget_bias, w_out, bq):
    d = x2.shape[1]
    (dsa_q, dsa_k, dsa_v, idx_q, idx_k, idx_w, gate_c, fox_q, fox_k, fox_v, fox_f, gate_d) = _split(
        w_in, (DSA_WIDTH, DSA_DIM, DSA_DIM, IDX_HEADS * IDX_DIM, IDX_DIM, IDX_HEADS, DSA_WIDTH,
               FOX_WIDTH, FOX_WIDTH, FOX_WIDTH, FOX_HEADS, FOX_WIDTH))
    w_big = _bf16_cols([gate_c, gate_d, dsa_q, fox_q, fox_k, fox_v, idx_q])
    pad = 3 * LANES - (2 * DSA_DIM + IDX_DIM + IDX_HEADS + FOX_HEADS)
    w_small = _bf16_cols([dsa_k, dsa_v, idx_k, idx_w, fox_f, jnp.zeros((d, pad), w_in.dtype)])

    h = _rmsnorm(x2, norm_g, BF16)
    proj = _matmul(h, w_big, BF16, tm=1024, tn=1024, name="odd_inproj")
    small = _matmul(h, w_small, F32, tm=1024, tn=3 * LANES, name="odd_small")

    rot = DSA_DIM // ROPE_FRACTION
    o_c = _dsa_attention(proj, small, _rope_tables(seq, rot, LANES, DSA_DIM ** -0.5), _rope_tables(seq, rot, LANES),
                         _rope_tables(seq, IDX_DIM // ROPE_FRACTION, IDX_DIM), batch, seq)
    cum = _logf_cumsum(small, forget_bias, batch, seq)
    nq = seq // bq
    cum_t = cum[:, FOX_F_LANE:FOX_F_LANE + FOX_HEADS].reshape(batch, seq, FOX_HEADS).transpose(0, 2, 1)
    cum_t = cum_t.reshape(batch * FOX_HEADS, nq, bq)
    o_d = _fox_attention(proj, cum, cum_t, batch, seq, bq=bq)
    return _outproj(o_c, o_d, w_out.astype(BF16), x2, tm=1024, tn=1024)


def kernel(x, even_norm, even_w_in, mla_q_norm, mla_w_uq, mla_kv_norm, mla_w_ukv, diff_lambda, diff_subln, even_w_out, odd_norm, odd_w_in, fox_forget_bias, odd_w_out, final_norm):
    batch, seq, d = x.shape
    bq = min(512, seq)
    h = x.reshape(batch * seq, d)
    depth = even_norm.shape[0] + odd_norm.shape[0]
    for layer in range(depth):
        i = layer // 2
        if layer % 2 == 0:
            lambda_init = 0.8 - 0.6 * math.exp(-0.3 * layer)
            h = _even_layer(h, batch, seq, even_norm[i], even_w_in[i], mla_q_norm[i], mla_w_uq[i], mla_kv_norm[i],
                            mla_w_ukv[i], diff_lambda[i], diff_subln[i], even_w_out[i], lambda_init, bq)
        else:
            h = _odd_layer(h, batch, seq, odd_norm[i], odd_w_in[i], fox_forget_bias[i], odd_w_out[i], bq)
    return _rmsnorm(h, final_norm, x.dtype).reshape(batch, seq, d)
```

```python
import functools
import math

import jax
import jax.numpy as jnp
from jax import lax
from jax.experimental import pallas as pl
from jax.experimental.pallas import tpu as pltpu

F32 = jnp.float32
BF16 = jnp.bfloat16

ROPE_THETA = 500000.0
NORM_EPS = 1e-6
ROPE_FRACTION = 4

MLA_HEADS, MLA_NOPE, MLA_ROPE, MLA_V = 16, 128, 64, 128
MLA_Q_RANK, MLA_KV_RANK = 1024, 512
DIFF_HEADS, DIFF_DIM = 8, 128
DSA_HEADS, DSA_DIM = 16, 128
IDX_HEADS, IDX_DIM = 16, 64
TOPK_MAX = 256
FOX_HEADS, FOX_DIM = 16, 128

MLA_WIDTH = MLA_HEADS * MLA_V
DIFF_WIDTH = DIFF_HEADS * 2 * DIFF_DIM
DSA_WIDTH = DSA_HEADS * DSA_DIM
FOX_WIDTH = FOX_HEADS * FOX_DIM
GATE_WIDTH = MLA_WIDTH + DIFF_WIDTH

LANES = 128
VMEM_LIMIT_BYTES = 56 * 1024 * 1024
MASKED = -1e30

IDX_W_LANE = IDX_DIM
FOX_F_LANE = IDX_DIM + IDX_HEADS

_NT = (((1,), (1,)), ((), ()))


def _params(*sem):
    return pltpu.CompilerParams(dimension_semantics=sem, vmem_limit_bytes=VMEM_LIMIT_BYTES)


def _rmsnorm_kernel(x_ref, g_ref, o_ref):
    x = x_ref[...].astype(F32)
    y = x * lax.rsqrt(jnp.mean(x * x, axis=-1, keepdims=True) + NORM_EPS)
    o_ref[...] = (y * g_ref[...]).astype(o_ref.dtype)


def _rmsnorm(x, g, out_dtype, tm=256):
    t, d = x.shape
    return pl.pallas_call(
        _rmsnorm_kernel,
        grid=(t // tm,),
        in_specs=[pl.BlockSpec((tm, d), lambda i: (i, 0)), pl.BlockSpec((1, d), lambda i: (0, 0))],
        out_specs=pl.BlockSpec((tm, d), lambda i: (i, 0)),
        out_shape=jax.ShapeDtypeStruct((t, d), out_dtype),
        compiler_params=_params("arbitrary"),
        name="rmsnorm",
    )(x, g.reshape(1, d).astype(F32))


def _mm_kernel(a_ref, b_ref, o_ref):
    o_ref[...] = jnp.dot(a_ref[...], b_ref[...], preferred_element_type=F32).astype(o_ref.dtype)


def _matmul(a, b, out_dtype, *, tm, tn, name):
    m, kdim = a.shape
    n = b.shape[1]
    return pl.pallas_call(
        _mm_kernel,
        grid=(m // tm, n // tn),
        in_specs=[pl.BlockSpec((tm, kdim), lambda i, j: (i, 0)),
                  pl.BlockSpec((kdim, tn), lambda i, j: (0, j))],
        out_specs=pl.BlockSpec((tm, tn), lambda i, j: (i, j)),
        out_shape=jax.ShapeDtypeStruct((m, n), out_dtype),
        compiler_params=_params("arbitrary", "arbitrary"),
        name=name,
    )(a, b)


def _norm_mm_kernel(a_ref, g_ref, b_ref, o_ref):
    x = a_ref[...].astype(F32)
    y = x * lax.rsqrt(jnp.mean(x * x, axis=-1, keepdims=True) + NORM_EPS) * g_ref[...]
    o_ref[...] = jnp.dot(y.astype(BF16), b_ref[...], preferred_element_type=F32).astype(o_ref.dtype)


def _norm_matmul(a, a_col_block, g, b, *, tm, tn, name):
    m = a.shape[0]
    kdim, n = b.shape
    return pl.pallas_call(
        _norm_mm_kernel,
        grid=(m // tm, n // tn),
        in_specs=[pl.BlockSpec((tm, kdim), lambda i, j: (i, a_col_block)),
                  pl.BlockSpec((1, kdim), lambda i, j: (0, 0)),
                  pl.BlockSpec((kdim, tn), lambda i, j: (0, j))],
        out_specs=pl.BlockSpec((tm, tn), lambda i, j: (i, j)),
        out_shape=jax.ShapeDtypeStruct((m, n), BF16),
        compiler_params=_params("arbitrary", "arbitrary"),
        name=name,
    )(a, g.reshape(1, kdim).astype(F32), b)


def _outproj_kernel(oa_ref, ob_ref, w_ref, res_ref, h_ref):
    ka = oa_ref.shape[1]
    acc = jnp.dot(oa_ref[...], w_ref[:ka, :], preferred_element_type=F32)
    acc = acc + jnp.dot(ob_ref[...], w_ref[ka:, :], preferred_element_type=F32)
    h_ref[...] = acc + res_ref[...]


def _outproj(oa, ob, w, res, *, tm, tn):
    m, ka = oa.shape
    kb = ob.shape[1]
    n = w.shape[1]
    return pl.pallas_call(
        _outproj_kernel,
        grid=(m // tm, n // tn),
        in_specs=[pl.BlockSpec((tm, ka), lambda i, j: (i, 0)),
                  pl.BlockSpec((tm, kb), lambda i, j: (i, 0)),
                  pl.BlockSpec((ka + kb, tn), lambda i, j: (0, j)),
                  pl.BlockSpec((tm, tn), lambda i, j: (i, j))],
        out_specs=pl.BlockSpec((tm, tn), lambda i, j: (i, j)),
        out_shape=jax.ShapeDtypeStruct((m, n), F32),
        compiler_params=_params("arbitrary", "arbitrary"),
        name="outproj",
    )(oa, ob, w, res)


def _rope_tables(seq, rot_dim, period, scale=1.0):
    inv = ROPE_THETA ** (-jnp.arange(0, rot_dim, 2, dtype=F32) / rot_dim)
    ang = jnp.arange(seq, dtype=F32)[:, None] * inv[None, :]
    cos, sin = jnp.cos(ang), jnp.sin(ang)
    ones = jnp.ones((seq, period - rot_dim), F32)
    c = jnp.concatenate([cos, cos, ones], axis=1) * scale
    s = jnp.concatenate([-sin, sin, 0.0 * ones], axis=1) * scale
    reps = LANES // period
    return jnp.tile(c, (1, reps)), jnp.tile(s, (1, reps))


def _rope(x, c, s, half, period):
    lane = lax.broadcasted_iota(jnp.int32, x.shape, 1)
    first = (lane & (period - 1)) < half
    partner = jnp.where(first, pltpu.roll(x, LANES - half, axis=1), pltpu.roll(x, half, axis=1))
    return x * c + partner * s


def _softmax_pv(s, v, carry, mask=None):
    m, l, acc = carry
    if mask is not None:
        s = jnp.where(mask, s, MASKED)
    m_new = jnp.maximum(m, jnp.max(s, axis=1, keepdims=True))
    alpha = jnp.exp(m - m_new)
    p = jnp.exp(s - m_new)
    l = alpha * l + jnp.sum(p, axis=1, keepdims=True)
    acc = alpha * acc + jnp.dot(p.astype(BF16), v, preferred_element_type=F32)
    return m_new, l, acc


def _init_carry(rows, dv):
    return (jnp.full((rows, 1), MASKED, F32), jnp.zeros((rows, 1), F32), jnp.zeros((rows, dv), F32))


FLASH_SUB = 512


def _tri_mask(n):
    row = lax.broadcasted_iota(jnp.int32, (n, n), 0)
    col = lax.broadcasted_iota(jnp.int32, (n, n), 1)
    return row >= col


def _causal_sweep(logits, values, i, bq, n_chain, chains_per_sub, dv):
    sub = FLASH_SUB
    nsub = bq // sub

    def step(j, st):
        v = values(j * nsub, nsub)
        return tuple(_softmax_pv(logits(t, j * nsub, nsub), v, st[t]) for t in range(n_chain))

    st = list(lax.fori_loop(0, i, step, tuple(_init_carry(sub, dv) for _ in range(n_chain))))
    mask = _tri_mask(sub)
    for c in range(nsub):
        blk = i * nsub + c
        v = values(blk, 1)
        for t in range(n_chain):
            r = t // chains_per_sub
            if r >= c:
                st[t] = _softmax_pv(logits(t, blk, 1), v, st[t], mask if r == c else None)
    return st


def _key_rows(blk, nb):
    return pl.ds(pl.multiple_of(blk * FLASH_SUB, FLASH_SUB), nb * FLASH_SUB)


def _silu(g):
    return g * jax.nn.sigmoid(g)


def _mla_kernel(q_ref, kn_ref, kr_ref, v_ref, gate_ref, cq_ref, sq_ref, ck_ref, sk_ref, o_ref, k_scr, *, bq, scale):
    i = pl.program_id(2)
    half = MLA_ROPE // 2
    sub = FLASH_SUB

    @pl.when(i == 0)
    def _():
        k_scr[:, :LANES] = kn_ref[...]
        k_scr[:, LANES:] = _rope(kr_ref[...].astype(F32), ck_ref[...], sk_ref[...], half, LANES).astype(BF16)

    qs = []
    for r in range(bq // sub):
        rs = slice(r * sub, (r + 1) * sub)
        qn = (q_ref[rs, :LANES].astype(F32) * scale).astype(BF16)
        qr = _rope(q_ref[rs, LANES:].astype(F32), cq_ref[rs, :], sq_ref[rs, :], half, LANES).astype(BF16)
        qs.append(jnp.concatenate([qn, qr], axis=1))

    def logits(t, blk, nb):
        return lax.dot_general(qs[t], k_scr[_key_rows(blk, nb), :], _NT, preferred_element_type=F32)

    def values(blk, nb):
        return v_ref[_key_rows(blk, nb), :]

    st = _causal_sweep(logits, values, i, bq, bq // sub, 1, MLA_V)
    for r, (_, l, acc) in enumerate(st):
        rs = slice(r * sub, (r + 1) * sub)
        o_ref[rs, :] = (acc / l * _silu(gate_ref[rs, :].astype(F32))).astype(o_ref.dtype)


def _mla_attention(q, kv, lat, proj, tabs_q, tabs_k, batch, seq, *, bq):
    t = batch * seq
    nq = seq // bq
    kr_block = (MLA_Q_RANK + MLA_KV_RANK) // LANES
    qrow = lambda b, h, i: (i, 0)
    full = lambda b, h, i: (0, 0)
    return pl.pallas_call(
        functools.partial(_mla_kernel, bq=bq, scale=(MLA_NOPE + MLA_ROPE) ** -0.5),
        grid=(batch, MLA_HEADS, nq),
        in_specs=[pl.BlockSpec((bq, 2 * LANES), lambda b, h, i: (b * nq + i, h)),
                  pl.BlockSpec((seq, LANES), lambda b, h, i: (b, h)),
                  pl.BlockSpec((seq, LANES), lambda b, h, i: (b, kr_block)),
                  pl.BlockSpec((seq, LANES), lambda b, h, i: (b, MLA_HEADS + h)),
                  pl.BlockSpec((bq, MLA_V), lambda b, h, i: (b * nq + i, h)),
                  pl.BlockSpec((bq, LANES), qrow), pl.BlockSpec((bq, LANES), qrow),
                  pl.BlockSpec((seq, LANES), full), pl.BlockSpec((seq, LANES), full)],
        out_specs=pl.BlockSpec((bq, MLA_V), lambda b, h, i: (b * nq + i, h)),
        out_shape=jax.ShapeDtypeStruct((t, MLA_WIDTH), BF16),
        scratch_shapes=[pltpu.VMEM((seq, 2 * LANES), BF16)],
        compiler_params=_params("arbitrary", "arbitrary", "arbitrary"),
        name="mla_attention",
    )(q, kv, lat, kv, proj, *tabs_q, *tabs_k)


def _diff_kernel(q_ref, k_ref, v_ref, gate_ref, lam_ref, g_ref, cq_ref, sq_ref, ck_ref, sk_ref, o_ref, k_scr,
                 *, bq, lambda_init):
    i = pl.program_id(2)
    half = DIFF_DIM // ROPE_FRACTION // 2
    sub = FLASH_SUB

    @pl.when(i == 0)
    def _():
        for c in range(2):
            sl = slice(c * LANES, (c + 1) * LANES)
            k_scr[:, sl] = _rope(k_ref[:, sl].astype(F32), ck_ref[...], sk_ref[...], half, LANES).astype(BF16)

    qs = []
    for r in range(bq // sub):
        rs = slice(r * sub, (r + 1) * sub)
        for c in range(2):
            x = q_ref[rs, c * LANES:(c + 1) * LANES].astype(F32)
            qs.append(_rope(x, cq_ref[rs, :], sq_ref[rs, :], half, LANES).astype(BF16))

    def logits(t, blk, nb):
        c = t % 2
        return lax.dot_general(qs[t], k_scr[_key_rows(blk, nb), c * LANES:(c + 1) * LANES], _NT,
                               preferred_element_type=F32)

    def values(blk, nb):
        return v_ref[_key_rows(blk, nb), :]

    st = _causal_sweep(logits, values, i, bq, 2 * (bq // sub), 2, 2 * DIFF_DIM)

    lp = lam_ref[...]
    lam = (jnp.exp(jnp.sum(lp[0:1] * lp[1:2], axis=1, keepdims=True))
           - jnp.exp(jnp.sum(lp[2:3] * lp[3:4], axis=1, keepdims=True)) + lambda_init)
    for r in range(bq // sub):
        rs = slice(r * sub, (r + 1) * sub)
        (_, l0, acc0), (_, l1, acc1) = st[2 * r], st[2 * r + 1]
        o = acc0 / l0 - lam * (acc1 / l1)
        o = o * lax.rsqrt(jnp.mean(o * o, axis=-1, keepdims=True) + NORM_EPS) * g_ref[...]
        o_ref[rs, :] = (o * (1.0 - lambda_init) * _silu(gate_ref[rs, :].astype(F32))).astype(o_ref.dtype)


def _diff_attention(proj, lam_params, subln_g, tabs_q, tabs_k, batch, seq, lambda_init, *, bq):
    t = batch * seq
    nq = seq // bq
    w = 2 * DIFF_DIM
    g0, q0, k0, v0 = MLA_WIDTH // w, GATE_WIDTH // w, (GATE_WIDTH + DIFF_WIDTH) // w, (GATE_WIDTH + 2 * DIFF_WIDTH) // w
    qrow = lambda b, h, i: (i, 0)
    full = lambda b, h, i: (0, 0)
    return pl.pallas_call(
        functools.partial(_diff_kernel, bq=bq, lambda_init=lambda_init),
        grid=(batch, DIFF_HEADS, nq),
        in_specs=[pl.BlockSpec((bq, w), lambda b, h, i: (b * nq + i, q0 + h)),
                  pl.BlockSpec((seq, w), lambda b, h, i: (b, k0 + h)),
                  pl.BlockSpec((seq, w), lambda b, h, i: (b, v0 + h)),
                  pl.BlockSpec((bq, w), lambda b, h, i: (b * nq + i, g0 + h)),
                  pl.BlockSpec((4, DIFF_DIM), full),
                  pl.BlockSpec((1, w), full),
                  pl.BlockSpec((bq, LANES), qrow), pl.BlockSpec((bq, LANES), qrow),
                  pl.BlockSpec((seq, LANES), full), pl.BlockSpec((seq, LANES), full)],
        out_specs=pl.BlockSpec((bq, w), lambda b, h, i: (b * nq + i, h)),
        out_shape=jax.ShapeDtypeStruct((t, DIFF_WIDTH), BF16),
        scratch_shapes=[pltpu.VMEM((seq, w), BF16)],
        compiler_params=_params("arbitrary", "arbitrary", "arbitrary"),
        name="diff_attention",
    )(proj, proj, proj, proj, lam_params.astype(F32), subln_g.reshape(1, w).astype(F32), *tabs_q, *tabs_k)


def _fox_kernel(q_ref, k_ref, v_ref, gate_ref, cum_ref, cumt_ref, o_ref, *, bq, scale):
    h = pl.program_id(1)
    i = pl.program_id(2)
    sub = FLASH_SUB
    lane = lax.broadcasted_iota(jnp.int32, (sub, LANES), 1)
    qs, cqs = [], []
    for r in range(bq // sub):
        rs = slice(r * sub, (r + 1) * sub)
        qs.append((q_ref[rs, :].astype(F32) * scale).astype(BF16))
        cqs.append(jnp.sum(jnp.where(lane == FOX_F_LANE + h, cum_ref[rs, :], 0.0), axis=1, keepdims=True))

    def logits(t, blk, nb):
        s = lax.dot_general(qs[t], k_ref[_key_rows(blk, nb), :], _NT, preferred_element_type=F32)
        ck = jnp.concatenate([cumt_ref[0, pl.ds(blk + n, 1), :] for n in range(nb)], axis=1)
        return s + (cqs[t] - ck)

    def values(blk, nb):
        return v_ref[_key_rows(blk, nb), :]

    st = _causal_sweep(logits, values, i, bq, bq // sub, 1, FOX_DIM)
    for r, (_, l, acc) in enumerate(st):
        rs = slice(r * sub, (r + 1) * sub)
        o_ref[rs, :] = (acc / l * _silu(gate_ref[rs, :].astype(F32))).astype(o_ref.dtype)


def _fox_attention(proj, cum, cum_t, batch, seq, *, bq):
    t = batch * seq
    nq = seq // bq
    g0 = DSA_WIDTH // LANES
    q0 = (GATE_WIDTH + DSA_WIDTH) // LANES
    k0, v0 = q0 + FOX_HEADS, q0 + 2 * FOX_HEADS
    return pl.pallas_call(
        functools.partial(_fox_kernel, bq=bq, scale=FOX_DIM ** -0.5),
        grid=(batch, FOX_HEADS, nq),
        in_specs=[pl.BlockSpec((bq, LANES), lambda b, h, i: (b * nq + i, q0 + h)),
                  pl.BlockSpec((seq, LANES), lambda b, h, i: (b, k0 + h)),
                  pl.BlockSpec((seq, LANES), lambda b, h, i: (b, v0 + h)),
                  pl.BlockSpec((bq, LANES), lambda b, h, i: (b * nq + i, g0 + h)),
                  pl.BlockSpec((bq, LANES), lambda b, h, i: (b * nq + i, 0)),
                  pl.BlockSpec((1, seq // FLASH_SUB, FLASH_SUB), lambda b, h, i: (b * FOX_HEADS + h, 0, 0))],
        out_specs=pl.BlockSpec((bq, FOX_DIM), lambda b, h, i: (b * nq + i, h)),
        out_shape=jax.ShapeDtypeStruct((t, FOX_WIDTH), BF16),
        compiler_params=_params("arbitrary", "arbitrary", "arbitrary"),
        name="fox_attention",
    )(proj, proj, proj, proj, cum, cum_t)


def _logf_cumsum_kernel(f_ref, b_ref, o_ref, carry_ref, *, tb):
    @pl.when(pl.program_id(1) == 0)
    def _():
        carry_ref[...] = jnp.zeros_like(carry_ref)

    x = f_ref[...] + b_ref[...]
    logf = jnp.minimum(x, 0.0) - jnp.log1p(jnp.exp(-jnp.abs(x)))
    row = lax.broadcasted_iota(jnp.int32, (tb, tb), 0)
    col = lax.broadcasted_iota(jnp.int32, (tb, tb), 1)
    tri = jnp.where(row >= col, 1.0, 0.0).astype(BF16)
    hi = logf.astype(BF16)
    r1 = logf - hi.astype(F32)
    mid = r1.astype(BF16)
    lo = (r1 - mid.astype(F32)).astype(BF16)
    cum = (jnp.dot(tri, hi, preferred_element_type=F32) + jnp.dot(tri, mid, preferred_element_type=F32)
           + jnp.dot(tri, lo, preferred_element_type=F32)) + carry_ref[...]
    o_ref[...] = cum
    carry_ref[...] = cum[tb - 1:tb, :]


def _logf_cumsum(small, forget_bias, batch, seq, *, tb=512):
    t = batch * seq
    nb = seq // tb
    bias = jnp.zeros((1, LANES), F32).at[0, FOX_F_LANE:FOX_F_LANE + FOX_HEADS].set(forget_bias.astype(F32))
    return pl.pallas_call(
        functools.partial(_logf_cumsum_kernel, tb=tb),
        grid=(batch, nb),
        in_specs=[pl.BlockSpec((tb, LANES), lambda b, i: (b * nb + i, 2)),
                  pl.BlockSpec((1, LANES), lambda b, i: (0, 0))],
        out_specs=pl.BlockSpec((tb, LANES), lambda b, i: (b * nb + i, 0)),
        out_shape=jax.ShapeDtypeStruct((t, LANES), F32),
        scratch_shapes=[pltpu.VMEM((1, LANES), F32)],
        compiler_params=_params("arbitrary", "arbitrary"),
        name="logf_cumsum",
    )(small, bias)


DSA_CK = 512
DSA_BQ = 128
DSA_GROUP = 4
DSA_MAX_ITERS = 640


def _row_total(x):
    return jnp.broadcast_to(jnp.sum(x, axis=1, keepdims=True), x.shape)


def _tile_lanes(x, n):
    return jnp.concatenate([x] * n, axis=1)


def _dsa_kernel(qc_ref, qi_ref, wq_ref, kv_ref, gate_ref, cdq_ref, sdq_ref, cdk_ref, sdk_ref, ciq_ref, siq_ref,
                cik_ref, sik_ref, o_ref, kc_scr, vc_scr, ki_scr, idx_scr, q_scr, qi_scr, w_scr, *, seq, topk):
    i = pl.program_id(1)
    bq, ck = DSA_BQ, DSA_CK
    dhalf = DSA_DIM // ROPE_FRACTION // 2
    ihalf = IDX_DIM // ROPE_FRACTION // 2
    rows = DSA_HEADS * bq
    lane = lax.broadcasted_iota(jnp.int32, (bq, LANES), 1)

    @pl.when(i == 0)
    def _():
        tb = 512

        def prep(r, _):
            off = pl.multiple_of(r * tb, tb)
            sl = pl.ds(off, tb)
            kc_scr[sl, :] = _rope(kv_ref[sl, 0:LANES], cdk_ref[sl, :], sdk_ref[sl, :], dhalf, LANES).astype(BF16)
            vc_scr[sl, :LANES] = kv_ref[sl, LANES:2 * LANES].astype(BF16)
            vc_scr[sl, LANES:] = jnp.ones((tb, LANES), BF16)
            lane_t = lax.broadcasted_iota(jnp.int32, (tb, LANES), 1)
            ki = jnp.where(lane_t < IDX_DIM, kv_ref[sl, 2 * LANES:3 * LANES], 0.0)
            ki = _rope(ki, cik_ref[sl, :], sik_ref[sl, :], ihalf, IDX_DIM)
            ki_scr[sl, :] = (ki + pltpu.roll(ki, IDX_DIM, axis=1)).astype(BF16)
            return 0

        lax.fori_loop(0, seq // tb, prep, 0)

    for h in range(DSA_HEADS):
        xh = qc_ref[:, h * LANES:(h + 1) * LANES].astype(F32)
        q_scr[h * bq:(h + 1) * bq, :] = _rope(xh, cdq_ref[...], sdq_ref[...], dhalf, LANES).astype(BF16)
    for p in range(IDX_HEADS // 2):
        xp = _rope(qi_ref[:, p * LANES:(p + 1) * LANES].astype(F32), ciq_ref[...], siq_ref[...], ihalf, IDX_DIM)
        qi_scr[(2 * p) * bq:(2 * p + 1) * bq, :] = jnp.where(lane < IDX_DIM, xp, 0.0).astype(BF16)
        qi_scr[(2 * p + 1) * bq:(2 * p + 2) * bq, :] = jnp.where(lane >= IDX_DIM, xp, 0.0).astype(BF16)
    w = wq_ref[:, 2 * LANES:3 * LANES] * ((IDX_HEADS ** -0.5) * (IDX_DIM ** -0.5))
    for h in range(IDX_HEADS):
        w_scr[h * bq:(h + 1) * bq, :] = _row_total(jnp.where(lane == IDX_W_LANE + h, w, 0.0))

    nch = i // (ck // bq) + 1
    qpos = i * bq + lax.broadcasted_iota(jnp.int32, (bq, LANES), 0)
    inf = jnp.full((bq, LANES), jnp.inf, F32)

    def idx_step(j, carry):
        mn, mx = carry
        off = pl.multiple_of(j * ck, ck)
        r = lax.dot_general(qi_scr[...], ki_scr[pl.ds(off, ck), :], _NT, preferred_element_type=F32)
        halves = []
        for c in range(ck // LANES):
            sc = None
            for h in range(IDX_HEADS):
                term = jnp.maximum(r[h * bq:(h + 1) * bq, c * LANES:(c + 1) * LANES], 0.0) * w_scr[h * bq:(h + 1) * bq, :]
                sc = term if sc is None else sc + term
            valid = (off + c * LANES + lane) <= qpos
            halves.append(jnp.where(valid, sc, -inf))
            mn = jnp.minimum(mn, jnp.where(valid, sc, inf))
            mx = jnp.maximum(mx, jnp.where(valid, sc, -inf))
        idx_scr[j] = jnp.concatenate(halves, axis=1)
        return mn, mx

    mn, mx = lax.fori_loop(0, nch, idx_step, (inf, -inf))
    mn = jnp.broadcast_to(jnp.min(mn, axis=1, keepdims=True), mn.shape)
    mx = jnp.broadcast_to(jnp.max(mx, axis=1, keepdims=True), mx.shape)

    def count_ge(t):
        def body(j, c):
            x = idx_scr[j]
            for half in range(ck // LANES):
                c = c + jnp.where(x[:, half * LANES:(half + 1) * LANES] >= t, 1.0, 0.0)
            return c
        return _row_total(lax.fori_loop(0, nch, body, jnp.zeros((bq, LANES), F32)))

    kf = float(topk)
    nvalid = (qpos + 1).astype(F32)
    c_mx = count_ge(mx)
    few = nvalid <= kf
    top_tie = (~few) & (c_mx >= kf)
    lo = jnp.where(top_tie, mx, mn)
    c_lo = jnp.where(top_tie, c_mx, nvalid)
    hi = jnp.where(top_tie, inf, mx)
    c_hi = jnp.where(top_tie, 0.0, c_mx)
    done = jnp.where(few | top_tie | (c_lo == kf), 1.0, 0.0)

    def search_cond(st):
        it, _, _, _, _, done = st
        return jnp.logical_and(it < DSA_MAX_ITERS, jnp.min(done) < 0.5)

    def search_body(st):
        it, lo, hi, c_lo, c_hi, done = st
        mid = 0.5 * lo + 0.5 * hi
        adjacent = (mid <= lo) | (mid >= hi)
        guess = lo + (hi - lo) * ((c_lo - kf - 0.5) / jnp.maximum(c_lo - c_hi, 1.0))
        use_guess = jnp.logical_and((it & 1) == 1, (guess > lo) & (guess < hi))
        t = jnp.where(use_guess, guess, mid)
        c = count_ge(t)
        ge = c >= kf
        upd = (done < 0.5) & (~adjacent)
        lo = jnp.where(upd & ge, t, lo)
        c_lo = jnp.where(upd & ge, c, c_lo)
        hi = jnp.where(upd & (~ge), t, hi)
        c_hi = jnp.where(upd & (~ge), c, c_hi)
        done = jnp.where(adjacent | (c_lo == kf), 1.0, done)
        return it + 1, lo, hi, c_lo, c_hi, done

    _, lo, hi, c_lo, c_hi, _ = lax.while_loop(search_cond, search_body, (jnp.int32(1), lo, hi, c_lo, c_hi, done))

    nl = ck // LANES
    lo_w = _tile_lanes(lo, nl)

    def write_plain():
        def body(j, _):
            idx_scr[j] = jnp.where(idx_scr[j] >= lo_w, 0.0, MASKED)
            return 0
        lax.fori_loop(0, nch, body, 0)

    def write_ties():
        hi_w = _tile_lanes(hi, nl)
        need = _tile_lanes(kf - c_hi, nl)
        urow = lax.broadcasted_iota(jnp.int32, (ck, ck), 0)
        ucol = lax.broadcasted_iota(jnp.int32, (ck, ck), 1)
        upper = jnp.where(urow < ucol, 1.0, 0.0).astype(BF16)

        def body(j, before):
            x = idx_scr[j]
            above = x >= hi_w
            eq = jnp.where((x >= lo_w) & (~above), 1.0, 0.0)
            prefix = jnp.dot(eq.astype(BF16), upper, preferred_element_type=F32) + _tile_lanes(before, nl)
            sel = above | ((eq > 0.5) & (prefix < need))
            idx_scr[j] = jnp.where(sel, 0.0, MASKED)
            tot = eq[:, :LANES]
            for c in range(1, nl):
                tot = tot + eq[:, c * LANES:(c + 1) * LANES]
            return before + _row_total(tot)

        lax.fori_loop(0, nch, body, jnp.zeros((bq, LANES), F32))

    lax.cond(jnp.max(c_lo) > kf, write_ties, write_plain)

    grp = DSA_GROUP * bq
    ngrp = rows // grp

    def att_step(j, carry):
        off = pl.multiple_of(j * ck, ck)
        k = kc_scr[pl.ds(off, ck), :]
        v = vc_scr[pl.ds(off, ck), :]
        bias = jnp.concatenate([idx_scr[j]] * DSA_GROUP, axis=0)
        out = []
        for g in range(ngrp):
            m, acc = carry[g]
            s = lax.dot_general(q_scr[g * grp:(g + 1) * grp, :], k, _NT, preferred_element_type=F32)
            parts = [s[:, c * LANES:(c + 1) * LANES] + bias[:, c * LANES:(c + 1) * LANES] for c in range(nl)]
            mc = parts[0]
            for c in range(1, nl):
                mc = jnp.maximum(mc, parts[c])
            m_new = jnp.maximum(m, jnp.max(mc, axis=1, keepdims=True))
            alpha = jnp.exp(m - m_new)
            p = jnp.concatenate([jnp.exp(x - m_new) for x in parts], axis=1).astype(BF16)
            acc = _tile_lanes(alpha, 2) * acc + jnp.dot(p, v, preferred_element_type=F32)
            out.append((m_new, acc))
        return tuple(out)

    init = tuple((jnp.full((grp, LANES), MASKED, F32), jnp.zeros((grp, 2 * LANES), F32)) for _ in range(ngrp))
    res = lax.fori_loop(0, nch, att_step, init)
    for g in range(ngrp):
        _, acc = res[g]
        o = acc[:, :LANES] / acc[:, LANES:]
        for t in range(DSA_GROUP):
            hl = slice((g * DSA_GROUP + t) * LANES, (g * DSA_GROUP + t + 1) * LANES)
            o_ref[:, hl] = (o[t * bq:(t + 1) * bq] * _silu(gate_ref[:, hl].astype(F32))).astype(o_ref.dtype)


def _dsa_attention(proj, small, dsa_tabs_q, dsa_tabs_k, idx_tabs, batch, seq):
    t = batch * seq
    bq = DSA_BQ
    nq = seq // bq
    topk = min(TOPK_MAX, seq // 4)
    rows = DSA_HEADS * bq
    idx_w = IDX_HEADS * IDX_DIM
    qrow = lambda b, i: (i, 0)
    full = lambda b, i: (0, 0)
    return pl.pallas_call(
        functools.partial(_dsa_kernel, seq=seq, topk=topk),
        grid=(batch, nq),
        in_specs=[pl.BlockSpec((bq, DSA_WIDTH), lambda b, i: (b * nq + i, GATE_WIDTH // DSA_WIDTH)),
                  pl.BlockSpec((bq, idx_w), lambda b, i: (b * nq + i, (GATE_WIDTH + DSA_WIDTH + 3 * FOX_WIDTH) // idx_w)),
                  pl.BlockSpec((bq, 3 * LANES), lambda b, i: (b * nq + i, 0)),
                  pl.BlockSpec((seq, 3 * LANES), lambda b, i: (b, 0)),
                  pl.BlockSpec((bq, DSA_WIDTH), lambda b, i: (b * nq + i, 0)),
                  pl.BlockSpec((bq, LANES), qrow), pl.BlockSpec((bq, LANES), qrow),
                  pl.BlockSpec((seq, LANES), full), pl.BlockSpec((seq, LANES), full),
                  pl.BlockSpec((bq, LANES), qrow), pl.BlockSpec((bq, LANES), qrow),
                  pl.BlockSpec((seq, LANES), full), pl.BlockSpec((seq, LANES), full)],
        out_specs=pl.BlockSpec((bq, DSA_WIDTH), lambda b, i: (b * nq + i, 0)),
        out_shape=jax.ShapeDtypeStruct((t, DSA_WIDTH), BF16),
        scratch_shapes=[pltpu.VMEM((seq, LANES), BF16), pltpu.VMEM((seq, 2 * LANES), BF16),
                        pltpu.VMEM((seq, LANES), BF16), pltpu.VMEM((seq // DSA_CK, bq, DSA_CK), F32),
                        pltpu.VMEM((rows, LANES), BF16), pltpu.VMEM((rows, LANES), BF16),
                        pltpu.VMEM((rows, LANES), F32)],
        compiler_params=_params("arbitrary", "arbitrary"),
        name="dsa_attention",
    )(proj, proj, small, small, proj, *dsa_tabs_q, *dsa_tabs_k, *idx_tabs, *idx_tabs)


def _split(w, sizes):
    out, acc = [], 0
    for s in sizes:
        out.append(w[:, acc:acc + s])
        acc += s
    return out


def _bf16_cols(parts):
    return jnp.concatenate([p.astype(BF16) for p in parts], axis=1)


def _even_layer(x2, batch, seq, norm_g, w_in, q_norm_g, w_uq, kv_norm_g, w_ukv, diff_lambda, subln_g, w_out,
                lambda_init, bq):
    d = x2.shape[1]
    c_q, c_kv, k_r, gate_a, dq, dk, dv, gate_b = _split(
        w_in, (MLA_Q_RANK, MLA_KV_RANK, MLA_ROPE, MLA_WIDTH, DIFF_WIDTH, DIFF_WIDTH, DIFF_WIDTH, DIFF_WIDTH))
    w_big = _bf16_cols([gate_a, gate_b, dq, dk, dv])
    w_lat = _bf16_cols([c_q, c_kv, k_r, jnp.zeros((d, LANES - MLA_ROPE), w_in.dtype)])
    uq = w_uq.astype(BF16).reshape(MLA_Q_RANK, MLA_HEADS, MLA_NOPE + MLA_ROPE)
    uq = jnp.concatenate([uq, jnp.zeros((MLA_Q_RANK, MLA_HEADS, 2 * LANES - MLA_NOPE - MLA_ROPE), BF16)], axis=2)
    uq = uq.reshape(MLA_Q_RANK, MLA_HEADS * 2 * LANES)
    ukv = w_ukv.astype(BF16).reshape(MLA_KV_RANK, MLA_HEADS, MLA_NOPE + MLA_V)
    ukv = jnp.concatenate([ukv[:, :, :MLA_NOPE].reshape(MLA_KV_RANK, -1), ukv[:, :, MLA_NOPE:].reshape(MLA_KV_RANK, -1)],
                          axis=1)

    h = _rmsnorm(x2, norm_g, BF16)
    proj = _matmul(h, w_big, BF16, tm=1024, tn=1024, name="even_inproj")
    lat = _matmul(h, w_lat, BF16, tm=512, tn=w_lat.shape[1], name="even_latent")
    q = _norm_matmul(lat, 0, q_norm_g, uq, tm=1024, tn=1024, name="mla_q_up")
    kv = _norm_matmul(lat, MLA_Q_RANK // MLA_KV_RANK, kv_norm_g, ukv, tm=1024, tn=1024, name="mla_kv_up")

    mla_scale = (MLA_NOPE + MLA_ROPE) ** -0.5
    o_a = _mla_attention(q, kv, lat, proj, _rope_tables(seq, MLA_ROPE, LANES, mla_scale),
                         _rope_tables(seq, MLA_ROPE, LANES), batch, seq, bq=bq)
    rot = DIFF_DIM // ROPE_FRACTION
    o_b = _diff_attention(proj, diff_lambda, subln_g, _rope_tables(seq, rot, LANES, DIFF_DIM ** -0.5),
                          _rope_tables(seq, rot, LANES), batch, seq, lambda_init, bq=bq)
    return _outproj(o_a, o_b, w_out.astype(BF16), x2, tm=1024, tn=1024)


def _odd_layer(x2, batch, seq, norm_g, w_in, forget_bias, w_out, bq):
    d = x2.shape[1]
    (dsa_q, dsa_k, dsa_v, idx_q, idx_k, idx_w, gate_c, fox_q, fox_k, fox_v, fox_f, gate_d) = _split(
        w_in, (DSA_WIDTH, DSA_DIM, DSA_DIM, IDX_HEADS * IDX_DIM, IDX_DIM, IDX_HEADS, DSA_WIDTH,
               FOX_WIDTH, FOX_WIDTH, FOX_WIDTH, FOX_HEADS, FOX_WIDTH))
    w_big = _bf16_cols([gate_c, gate_d, dsa_q, fox_q, fox_k, fox_v, idx_q])
    pad = 3 * LANES - (2 * DSA_DIM + IDX_DIM + IDX_HEADS + FOX_HEADS)
    w_small = _bf16_cols([dsa_k, dsa_v, idx_k, idx_w, fox_f, jnp.zeros((d, pad), w_in.dtype)])

    h = _rmsnorm(x2, norm_g, BF16)
    proj = _matmul(h, w_big, BF16, tm=1024, tn=1024, name="odd_inproj")
    small = _matmul(h, w_small, F32, tm=1024, tn=3 * LANES, name="odd_small")

    rot = DSA_DIM // ROPE_FRACTION
    o_c = _dsa_attention(proj, small, _rope_tables(seq, rot, LANES, DSA_DIM ** -0.5), _rope_tables(seq, rot, LANES),
                         _rope_tables(seq, IDX_DIM // ROPE_FRACTION, IDX_DIM), batch, seq)
    cum = _logf_cumsum(small, forget_bias, batch, seq)
    cum_t = cum[:, FOX_F_LANE:FOX_F_LANE + FOX_HEADS].reshape(batch, seq, FOX_HEADS).transpose(0, 2, 1)
    cum_t = cum_t.reshape(batch * FOX_HEADS, seq // FLASH_SUB, FLASH_SUB)
    o_d = _fox_attention(proj, cum, cum_t, batch, seq, bq=bq)
    return _outproj(o_c, o_d, w_out.astype(BF16), x2, tm=1024, tn=1024)


def kernel(x, even_norm, even_w_in, mla_q_norm, mla_w_uq, mla_kv_norm, mla_w_ukv, diff_lambda, diff_subln, even_w_out, odd_norm, odd_w_in, fox_forget_bias, odd_w_out, final_norm):
    batch, seq, d = x.shape
    bq = min(1024, seq)
    h = x.reshape(batch * seq, d)
    depth = even_norm.shape[0] + odd_norm.shape[0]
    for layer in range(depth):
        i = layer // 2
        if layer % 2 == 0:
            lambda_init = 0.8 - 0.6 * math.exp(-0.3 * layer)
            h = _even_layer(h, batch, seq, even_norm[i], even_w_in[i], mla_q_norm[i], mla_w_uq[i], mla_kv_norm[i],
                            mla_w_ukv[i], diff_lambda[i], diff_subln[i], even_w_out[i], lambda_init, bq)
        else:
            h = _odd_layer(h, batch, seq, odd_norm[i], odd_w_in[i], fox_forget_bias[i], odd_w_out[i], bq)
    return _rmsnorm(h, final_norm, x.dtype).reshape(batch, seq, d)
```

```python
import functools
import math

import jax
import jax.numpy as jnp
from jax import lax
from jax.experimental import pallas as pl
from jax.experimental.pallas import tpu as pltpu

F32 = jnp.float32
BF16 = jnp.bfloat16

ROPE_THETA = 500000.0
NORM_EPS = 1e-6
ROPE_FRACTION = 4

MLA_HEADS, MLA_NOPE, MLA_ROPE, MLA_V = 16, 128, 64, 128
MLA_Q_RANK, MLA_KV_RANK = 1024, 512
DIFF_HEADS, DIFF_DIM = 8, 128
DSA_HEADS, DSA_DIM = 16, 128
IDX_HEADS, IDX_DIM = 16, 64
TOPK_MAX = 256
FOX_HEADS, FOX_DIM = 16, 128

MLA_WIDTH = MLA_HEADS * MLA_V
DIFF_WIDTH = DIFF_HEADS * 2 * DIFF_DIM
DSA_WIDTH = DSA_HEADS * DSA_DIM
FOX_WIDTH = FOX_HEADS * FOX_DIM

LANES = 128
VMEM_LIMIT_BYTES = 56 * 1024 * 1024
MASKED = -1e30

IDX_W_LANE = IDX_DIM
FOX_F_LANE = IDX_DIM + IDX_HEADS

_NT = (((1,), (1,)), ((), ()))


def _params(*sem):
    return pltpu.CompilerParams(dimension_semantics=sem, vmem_limit_bytes=VMEM_LIMIT_BYTES)


def _rmsnorm_kernel(x_ref, g_ref, o_ref):
    x = x_ref[...].astype(F32)
    y = x * lax.rsqrt(jnp.mean(x * x, axis=-1, keepdims=True) + NORM_EPS)
    o_ref[...] = (y * g_ref[...]).astype(o_ref.dtype)


def _rmsnorm(x, g, out_dtype, tm=256):
    t, d = x.shape
    return pl.pallas_call(
        _rmsnorm_kernel,
        grid=(t // tm,),
        in_specs=[pl.BlockSpec((tm, d), lambda i: (i, 0)), pl.BlockSpec((1, d), lambda i: (0, 0))],
        out_specs=pl.BlockSpec((tm, d), lambda i: (i, 0)),
        out_shape=jax.ShapeDtypeStruct((t, d), out_dtype),
        compiler_params=_params("arbitrary"),
        name="rmsnorm",
    )(x, g.reshape(1, d).astype(F32))


def _mm_kernel(a_ref, b_ref, o_ref):
    o_ref[...] = jnp.dot(a_ref[...], b_ref[...], preferred_element_type=F32).astype(o_ref.dtype)


def _matmul(a, b, out_dtype, *, tm, tn, name):
    m, kdim = a.shape
    n = b.shape[1]
    return pl.pallas_call(
        _mm_kernel,
        grid=(m // tm, n // tn),
        in_specs=[pl.BlockSpec((tm, kdim), lambda i, j: (i, 0)),
                  pl.BlockSpec((kdim, tn), lambda i, j: (0, j))],
        out_specs=pl.BlockSpec((tm, tn), lambda i, j: (i, j)),
        out_shape=jax.ShapeDtypeStruct((m, n), out_dtype),
        compiler_params=_params("arbitrary", "arbitrary"),
        name=name,
    )(a, b)


def _norm_mm_kernel(a_ref, g_ref, b_ref, o_ref):
    x = a_ref[...].astype(F32)
    y = x * lax.rsqrt(jnp.mean(x * x, axis=-1, keepdims=True) + NORM_EPS) * g_ref[...]
    o_ref[...] = jnp.dot(y.astype(BF16), b_ref[...], preferred_element_type=F32).astype(o_ref.dtype)


def _norm_matmul(a, a_col_block, g, b, *, tm, tn, name):
    m = a.shape[0]
    kdim, n = b.shape
    return pl.pallas_call(
        _norm_mm_kernel,
        grid=(m // tm, n // tn),
        in_specs=[pl.BlockSpec((tm, kdim), lambda i, j: (i, a_col_block)),
                  pl.BlockSpec((1, kdim), lambda i, j: (0, 0)),
                  pl.BlockSpec((kdim, tn), lambda i, j: (0, j))],
        out_specs=pl.BlockSpec((tm, tn), lambda i, j: (i, j)),
        out_shape=jax.ShapeDtypeStruct((m, n), BF16),
        compiler_params=_params("arbitrary", "arbitrary"),
        name=name,
    )(a, g.reshape(1, kdim).astype(F32), b)


def _outproj_kernel(oa_ref, ob_ref, w_ref, res_ref, h_ref):
    ka = oa_ref.shape[1]
    acc = jnp.dot(oa_ref[...], w_ref[:ka, :], preferred_element_type=F32)
    acc = acc + jnp.dot(ob_ref[...], w_ref[ka:, :], preferred_element_type=F32)
    h_ref[...] = acc + res_ref[...]


def _outproj(oa, ob, w, res, *, tm, tn):
    m, ka = oa.shape
    kb = ob.shape[1]
    n = w.shape[1]
    return pl.pallas_call(
        _outproj_kernel,
        grid=(m // tm, n // tn),
        in_specs=[pl.BlockSpec((tm, ka), lambda i, j: (i, 0)),
                  pl.BlockSpec((tm, kb), lambda i, j: (i, 0)),
                  pl.BlockSpec((ka + kb, tn), lambda i, j: (0, j)),
                  pl.BlockSpec((tm, tn), lambda i, j: (i, j))],
        out_specs=pl.BlockSpec((tm, tn), lambda i, j: (i, j)),
        out_shape=jax.ShapeDtypeStruct((m, n), F32),
        compiler_params=_params("arbitrary", "arbitrary"),
        name="outproj",
    )(oa, ob, w, res)


def _rope_tables(seq, rot_dim, period, scale=1.0):
    inv = ROPE_THETA ** (-jnp.arange(0, rot_dim, 2, dtype=F32) / rot_dim)
    ang = jnp.arange(seq, dtype=F32)[:, None] * inv[None, :]
    cos, sin = jnp.cos(ang), jnp.sin(ang)
    ones = jnp.ones((seq, period - rot_dim), F32)
    c = jnp.concatenate([cos, cos, ones], axis=1) * scale
    s = jnp.concatenate([-sin, sin, 0.0 * ones], axis=1) * scale
    reps = LANES // period
    return jnp.tile(c, (1, reps)), jnp.tile(s, (1, reps))


def _rope(x, c, s, half, period):
    lane = lax.broadcasted_iota(jnp.int32, x.shape, 1)
    first = (lane & (period - 1)) < half
    partner = jnp.where(first, pltpu.roll(x, LANES - half, axis=1), pltpu.roll(x, half, axis=1))
    return x * c + partner * s


def _softmax_pv(s, v, carry, mask=None):
    m, l, acc = carry
    if mask is not None:
        s = jnp.where(mask, s, MASKED)
    m_new = jnp.maximum(m, jnp.max(s, axis=1, keepdims=True))
    alpha = jnp.exp(m - m_new)
    p = jnp.exp(s - m_new)
    l = alpha * l + jnp.sum(p, axis=1, keepdims=True)
    acc = alpha * acc + jnp.dot(p.astype(BF16), v, preferred_element_type=F32)
    return m_new, l, acc


def _init_carry(rows, dv):
    return (jnp.full((rows, 1), MASKED, F32), jnp.zeros((rows, 1), F32), jnp.zeros((rows, dv), F32))


FLASH_SUB = 512


def _tri_mask(n):
    row = lax.broadcasted_iota(jnp.int32, (n, n), 0)
    col = lax.broadcasted_iota(jnp.int32, (n, n), 1)
    return row >= col


def _causal_sweep(logits, values, i, bq, n_chain, chains_per_sub, dv):
    sub = FLASH_SUB
    nsub = bq // sub

    def step(j, st):
        v = values(j * nsub, nsub)
        return tuple(_softmax_pv(logits(t, j * nsub, nsub), v, st[t]) for t in range(n_chain))

    st = list(lax.fori_loop(0, i, step, tuple(_init_carry(sub, dv) for _ in range(n_chain))))
    mask = _tri_mask(sub)
    for c in range(nsub):
        blk = i * nsub + c
        v = values(blk, 1)
        for t in range(n_chain):
            r = t // chains_per_sub
            if r >= c:
                st[t] = _softmax_pv(logits(t, blk, 1), v, st[t], mask if r == c else None)
    return st


def _key_rows(blk, nb):
    return pl.ds(pl.multiple_of(blk * FLASH_SUB, FLASH_SUB), nb * FLASH_SUB)


def _silu(g):
    return g * jax.nn.sigmoid(g)


def _mla_kernel(q_ref, kn_ref, kr_ref, v_ref, gate_ref, cq_ref, sq_ref, ck_ref, sk_ref, o_ref, k_scr, *, bq, scale):
    i = pl.program_id(2)
    half = MLA_ROPE // 2
    sub = FLASH_SUB

    @pl.when(i == 0)
    def _():
        k_scr[:, :LANES] = kn_ref[...]
        k_scr[:, LANES:] = _rope(kr_ref[...].astype(F32), ck_ref[...], sk_ref[...], half, LANES).astype(BF16)

    qs = []
    for r in range(bq // sub):
        rs = slice(r * sub, (r + 1) * sub)
        qn = (q_ref[rs, :LANES].astype(F32) * scale).astype(BF16)
        qr = _rope(q_ref[rs, LANES:].astype(F32), cq_ref[rs, :], sq_ref[rs, :], half, LANES).astype(BF16)
        qs.append(jnp.concatenate([qn, qr], axis=1))

    def logits(t, blk, nb):
        return lax.dot_general(qs[t], k_scr[_key_rows(blk, nb), :], _NT, preferred_element_type=F32)

    def values(blk, nb):
        return v_ref[_key_rows(blk, nb), :]

    st = _causal_sweep(logits, values, i, bq, bq // sub, 1, MLA_V)
    for r, (_, l, acc) in enumerate(st):
        rs = slice(r * sub, (r + 1) * sub)
        o_ref[rs, :] = (acc / l * _silu(gate_ref[rs, :].astype(F32))).astype(o_ref.dtype)


def _mla_attention(q, kv, lat, proj, tabs_q, tabs_k, batch, seq, *, bq):
    t = batch * seq
    nq = seq // bq
    kr_block = (MLA_Q_RANK + MLA_KV_RANK) // LANES
    qrow = lambda b, h, i: (i, 0)
    full = lambda b, h, i: (0, 0)
    return pl.pallas_call(
        functools.partial(_mla_kernel, bq=bq, scale=(MLA_NOPE + MLA_ROPE) ** -0.5),
        grid=(batch, MLA_HEADS, nq),
        in_specs=[pl.BlockSpec((bq, 2 * LANES), lambda b, h, i: (b * nq + i, h)),
                  pl.BlockSpec((seq, LANES), lambda b, h, i: (b, h)),
                  pl.BlockSpec((seq, LANES), lambda b, h, i: (b, kr_block)),
                  pl.BlockSpec((seq, LANES), lambda b, h, i: (b, MLA_HEADS + h)),
                  pl.BlockSpec((bq, MLA_V), lambda b, h, i: (b * nq + i, h)),
                  pl.BlockSpec((bq, LANES), qrow), pl.BlockSpec((bq, LANES), qrow),
                  pl.BlockSpec((seq, LANES), full), pl.BlockSpec((seq, LANES), full)],
        out_specs=pl.BlockSpec((bq, MLA_V), lambda b, h, i: (b * nq + i, h)),
        out_shape=jax.ShapeDtypeStruct((t, MLA_WIDTH), BF16),
        scratch_shapes=[pltpu.VMEM((seq, 2 * LANES), BF16)],
        compiler_params=_params("arbitrary", "arbitrary", "arbitrary"),
        name="mla_attention",
    )(q, kv, lat, kv, proj, *tabs_q, *tabs_k)


def _diff_kernel(q_ref, k_ref, v_ref, gate_ref, lam_ref, g_ref, cq_ref, sq_ref, ck_ref, sk_ref, o_ref, k_scr,
                 *, bq, lambda_init):
    i = pl.program_id(2)
    half = DIFF_DIM // ROPE_FRACTION // 2
    sub = FLASH_SUB

    @pl.when(i == 0)
    def _():
        for c in range(2):
            sl = slice(c * LANES, (c + 1) * LANES)
            k_scr[:, sl] = _rope(k_ref[:, sl].astype(F32), ck_ref[...], sk_ref[...], half, LANES).astype(BF16)

    qs = []
    for r in range(bq // sub):
        rs = slice(r * sub, (r + 1) * sub)
        for c in range(2):
            x = q_ref[rs, c * LANES:(c + 1) * LANES].astype(F32)
            qs.append(_rope(x, cq_ref[rs, :], sq_ref[rs, :], half, LANES).astype(BF16))

    def logits(t, blk, nb):
        c = t % 2
        return lax.dot_general(qs[t], k_scr[_key_rows(blk, nb), c * LANES:(c + 1) * LANES], _NT,
                               preferred_element_type=F32)

    def values(blk, nb):
        return v_ref[_key_rows(blk, nb), :]

    st = _causal_sweep(logits, values, i, bq, 2 * (bq // sub), 2, 2 * DIFF_DIM)

    lp = lam_ref[...]
    lam = (jnp.exp(jnp.sum(lp[0:1] * lp[1:2], axis=1, keepdims=True))
           - jnp.exp(jnp.sum(lp[2:3] * lp[3:4], axis=1, keepdims=True)) + lambda_init)
    for r in range(bq // sub):
        rs = slice(r * sub, (r + 1) * sub)
        (_, l0, acc0), (_, l1, acc1) = st[2 * r], st[2 * r + 1]
        o = acc0 / l0 - lam * (acc1 / l1)
        o = o * lax.rsqrt(jnp.mean(o * o, axis=-1, keepdims=True) + NORM_EPS) * g_ref[...]
        o_ref[rs, :] = (o * (1.0 - lambda_init) * _silu(gate_ref[rs, :].astype(F32))).astype(o_ref.dtype)


def _diff_attention(proj, lam_params, subln_g, tabs_q, tabs_k, batch, seq, lambda_init, *, bq):
    t = batch * seq
    nq = seq // bq
    w = 2 * DIFF_DIM
    q0 = MLA_WIDTH // w
    k0, v0, g0 = q0 + DIFF_HEADS, q0 + 2 * DIFF_HEADS, q0 + 3 * DIFF_HEADS
    qrow = lambda b, h, i: (i, 0)
    full = lambda b, h, i: (0, 0)
    return pl.pallas_call(
        functools.partial(_diff_kernel, bq=bq, lambda_init=lambda_init),
        grid=(batch, DIFF_HEADS, nq),
        in_specs=[pl.BlockSpec((bq, w), lambda b, h, i: (b * nq + i, q0 + h)),
                  pl.BlockSpec((seq, w), lambda b, h, i: (b, k0 + h)),
                  pl.BlockSpec((seq, w), lambda b, h, i: (b, v0 + h)),
                  pl.BlockSpec((bq, w), lambda b, h, i: (b * nq + i, g0 + h)),
                  pl.BlockSpec((4, DIFF_DIM), full),
                  pl.BlockSpec((1, w), full),
                  pl.BlockSpec((bq, LANES), qrow), pl.BlockSpec((bq, LANES), qrow),
                  pl.BlockSpec((seq, LANES), full), pl.BlockSpec((seq, LANES), full)],
        out_specs=pl.BlockSpec((bq, w), lambda b, h, i: (b * nq + i, h)),
        out_shape=jax.ShapeDtypeStruct((t, DIFF_WIDTH), BF16),
        scratch_shapes=[pltpu.VMEM((seq, w), BF16)],
        compiler_params=_params("arbitrary", "arbitrary", "arbitrary"),
        name="diff_attention",
    )(proj, proj, proj, proj, lam_params.astype(F32), subln_g.reshape(1, w).astype(F32), *tabs_q, *tabs_k)


def _fox_kernel(q_ref, k_ref, v_ref, gate_ref, cum_ref, cumt_ref, o_ref, *, bq, scale):
    h = pl.program_id(1)
    i = pl.program_id(2)
    sub = FLASH_SUB
    lane = lax.broadcasted_iota(jnp.int32, (sub, LANES), 1)
    qs, cqs = [], []
    for r in range(bq // sub):
        rs = slice(r * sub, (r + 1) * sub)
        qs.append((q_ref[rs, :].astype(F32) * scale).astype(BF16))
        cqs.append(jnp.sum(jnp.where(lane == FOX_F_LANE + h, cum_ref[rs, :], 0.0), axis=1, keepdims=True))

    def logits(t, blk, nb):
        s = lax.dot_general(qs[t], k_ref[_key_rows(blk, nb), :], _NT, preferred_element_type=F32)
        ck = jnp.concatenate([cumt_ref[0, pl.ds(blk + n, 1), :] for n in range(nb)], axis=1)
        return s + (cqs[t] - ck)

    def values(blk, nb):
        return v_ref[_key_rows(blk, nb), :]

    st = _causal_sweep(logits, values, i, bq, bq // sub, 1, FOX_DIM)
    for r, (_, l, acc) in enumerate(st):
        rs = slice(r * sub, (r + 1) * sub)
        o_ref[rs, :] = (acc / l * _silu(gate_ref[rs, :].astype(F32))).astype(o_ref.dtype)


def _fox_attention(proj, cum, cum_t, batch, seq, *, bq):
    t = batch * seq
    nq = seq // bq
    q0 = DSA_WIDTH // LANES
    k0, v0, g0 = q0 + FOX_HEADS, q0 + 2 * FOX_HEADS, q0 + 3 * FOX_HEADS
    return pl.pallas_call(
        functools.partial(_fox_kernel, bq=bq, scale=FOX_DIM ** -0.5),
        grid=(batch, FOX_HEADS, nq),
        in_specs=[pl.BlockSpec((bq, LANES), lambda b, h, i: (b * nq + i, q0 + h)),
                  pl.BlockSpec((seq, LANES), lambda b, h, i: (b, k0 + h)),
                  pl.BlockSpec((seq, LANES), lambda b, h, i: (b, v0 + h)),
                  pl.BlockSpec((bq, LANES), lambda b, h, i: (b * nq + i, g0 + h)),
                  pl.BlockSpec((bq, LANES), lambda b, h, i: (b * nq + i, 0)),
                  pl.BlockSpec((1, seq // FLASH_SUB, FLASH_SUB), lambda b, h, i: (b * FOX_HEADS + h, 0, 0))],
        out_specs=pl.BlockSpec((bq, FOX_DIM), lambda b, h, i: (b * nq + i, h)),
        out_shape=jax.ShapeDtypeStruct((t, FOX_WIDTH), BF16),
        compiler_params=_params("arbitrary", "arbitrary", "arbitrary"),
        name="fox_attention",
    )(proj, proj, proj, proj, cum, cum_t)


def _logf_cumsum_kernel(f_ref, b_ref, o_ref, carry_ref, *, tb):
    @pl.when(pl.program_id(1) == 0)
    def _():
        carry_ref[...] = jnp.zeros_like(carry_ref)

    x = f_ref[...] + b_ref[...]
    logf = jnp.minimum(x, 0.0) - jnp.log1p(jnp.exp(-jnp.abs(x)))
    row = lax.broadcasted_iota(jnp.int32, (tb, tb), 0)
    col = lax.broadcasted_iota(jnp.int32, (tb, tb), 1)
    tri = jnp.where(row >= col, 1.0, 0.0).astype(BF16)
    hi = logf.astype(BF16)
    r1 = logf - hi.astype(F32)
    mid = r1.astype(BF16)
    lo = (r1 - mid.astype(F32)).astype(BF16)
    cum = (jnp.dot(tri, hi, preferred_element_type=F32) + jnp.dot(tri, mid, preferred_element_type=F32)
           + jnp.dot(tri, lo, preferred_element_type=F32)) + carry_ref[...]
    o_ref[...] = cum
    carry_ref[...] = cum[tb - 1:tb, :]


def _logf_cumsum(small, forget_bias, batch, seq, *, tb=512):
    t = batch * seq
    nb = seq // tb
    bias = jnp.zeros((1, LANES), F32).at[0, FOX_F_LANE:FOX_F_LANE + FOX_HEADS].set(forget_bias.astype(F32))
    return pl.pallas_call(
        functools.partial(_logf_cumsum_kernel, tb=tb),
        grid=(batch, nb),
        in_specs=[pl.BlockSpec((tb, LANES), lambda b, i: (b * nb + i, 2)),
                  pl.BlockSpec((1, LANES), lambda b, i: (0, 0))],
        out_specs=pl.BlockSpec((tb, LANES), lambda b, i: (b * nb + i, 0)),
        out_shape=jax.ShapeDtypeStruct((t, LANES), F32),
        scratch_shapes=[pltpu.VMEM((1, LANES), F32)],
        compiler_params=_params("arbitrary", "arbitrary"),
        name="logf_cumsum",
    )(small, bias)


DSA_CK = 512
DSA_BQ = 128
DSA_GROUP = 4
DSA_MAX_ITERS = 640
DSA_PROBES_PER_CHECK = 4


def _row_total(x):
    return jnp.broadcast_to(jnp.sum(x, axis=1, keepdims=True), x.shape)


def _tile_lanes(x, n):
    return jnp.concatenate([x] * n, axis=1)


def _dsa_kernel(qc_ref, qi_ref, wq_ref, kv_ref, gate_ref, cdq_ref, sdq_ref, cdk_ref, sdk_ref, ciq_ref, siq_ref,
                cik_ref, sik_ref, o_ref, kc_scr, vc_scr, ki_scr, idx_scr, q_scr, qi_scr, w_scr, *, seq, topk):
    i = pl.program_id(1)
    bq, ck = DSA_BQ, DSA_CK
    dhalf = DSA_DIM // ROPE_FRACTION // 2
    ihalf = IDX_DIM // ROPE_FRACTION // 2
    rows = DSA_HEADS * bq
    lane = lax.broadcasted_iota(jnp.int32, (bq, LANES), 1)

    @pl.when(i == 0)
    def _():
        tb = 512

        def prep(r, _):
            off = pl.multiple_of(r * tb, tb)
            sl = pl.ds(off, tb)
            kc_scr[sl, :] = _rope(kv_ref[sl, 0:LANES], cdk_ref[sl, :], sdk_ref[sl, :], dhalf, LANES).astype(BF16)
            vc_scr[sl, :LANES] = kv_ref[sl, LANES:2 * LANES].astype(BF16)
            vc_scr[sl, LANES:] = jnp.ones((tb, LANES), BF16)
            lane_t = lax.broadcasted_iota(jnp.int32, (tb, LANES), 1)
            ki = jnp.where(lane_t < IDX_DIM, kv_ref[sl, 2 * LANES:3 * LANES], 0.0)
            ki = _rope(ki, cik_ref[sl, :], sik_ref[sl, :], ihalf, IDX_DIM)
            ki_scr[sl, :] = (ki + pltpu.roll(ki, IDX_DIM, axis=1)).astype(BF16)
            return 0

        lax.fori_loop(0, seq // tb, prep, 0)

    for h in range(DSA_HEADS):
        xh = qc_ref[:, h * LANES:(h + 1) * LANES].astype(F32)
        q_scr[h * bq:(h + 1) * bq, :] = _rope(xh, cdq_ref[...], sdq_ref[...], dhalf, LANES).astype(BF16)
    for p in range(IDX_HEADS // 2):
        xp = _rope(qi_ref[:, p * LANES:(p + 1) * LANES].astype(F32), ciq_ref[...], siq_ref[...], ihalf, IDX_DIM)
        qi_scr[(2 * p) * bq:(2 * p + 1) * bq, :] = jnp.where(lane < IDX_DIM, xp, 0.0).astype(BF16)
        qi_scr[(2 * p + 1) * bq:(2 * p + 2) * bq, :] = jnp.where(lane >= IDX_DIM, xp, 0.0).astype(BF16)
    w = wq_ref[:, 2 * LANES:3 * LANES] * ((IDX_HEADS ** -0.5) * (IDX_DIM ** -0.5))
    for h in range(IDX_HEADS):
        w_scr[h * bq:(h + 1) * bq, :] = _row_total(jnp.where(lane == IDX_W_LANE + h, w, 0.0))

    nch = i // (ck // bq) + 1
    qpos = i * bq + lax.broadcasted_iota(jnp.int32, (bq, LANES), 0)
    inf = jnp.full((bq, LANES), jnp.inf, F32)

    def idx_step(j, carry):
        mn, mx = carry
        off = pl.multiple_of(j * ck, ck)
        r = lax.dot_general(qi_scr[...], ki_scr[pl.ds(off, ck), :], _NT, preferred_element_type=F32)
        halves = []
        for c in range(ck // LANES):
            sc = None
            for h in range(IDX_HEADS):
                term = jnp.maximum(r[h * bq:(h + 1) * bq, c * LANES:(c + 1) * LANES], 0.0) * w_scr[h * bq:(h + 1) * bq, :]
                sc = term if sc is None else sc + term
            valid = (off + c * LANES + lane) <= qpos
            halves.append(jnp.where(valid, sc, -inf))
            mn = jnp.minimum(mn, jnp.where(valid, sc, inf))
            mx = jnp.maximum(mx, jnp.where(valid, sc, -inf))
        idx_scr[j] = jnp.concatenate(halves, axis=1)
        return mn, mx

    mn, mx = lax.fori_loop(0, nch, idx_step, (inf, -inf))
    mn = jnp.broadcast_to(jnp.min(mn, axis=1, keepdims=True), mn.shape)
    mx = jnp.broadcast_to(jnp.max(mx, axis=1, keepdims=True), mx.shape)

    def count_ge(t):
        def body(j, c):
            x = idx_scr[j]
            for half in range(ck // LANES):
                c = c + jnp.where(x[:, half * LANES:(half + 1) * LANES] >= t, 1.0, 0.0)
            return c
        return _row_total(lax.fori_loop(0, nch, body, jnp.zeros((bq, LANES), F32)))

    kf = float(topk)
    nvalid = (qpos + 1).astype(F32)
    c_mx = count_ge(mx)
    few = nvalid <= kf
    top_tie = (~few) & (c_mx >= kf)
    lo = jnp.where(top_tie, mx, mn)
    c_lo = jnp.where(top_tie, c_mx, nvalid)
    hi = jnp.where(top_tie, inf, mx)
    c_hi = jnp.where(top_tie, 0.0, c_mx)
    done = jnp.where(few | top_tie | (c_lo == kf), 1.0, 0.0)

    def search_cond(st):
        it, _, _, _, _, done = st
        return jnp.logical_and(it < DSA_MAX_ITERS, jnp.min(done) < 0.5)

    def probe(st):
        lo, hi, c_lo, c_hi, done = st
        mid = 0.5 * lo + 0.5 * hi
        adjacent = (mid <= lo) | (mid >= hi)
        c = count_ge(mid)
        ge = c >= kf
        upd = (done < 0.5) & (~adjacent)
        lo = jnp.where(upd & ge, mid, lo)
        c_lo = jnp.where(upd & ge, c, c_lo)
        hi = jnp.where(upd & (~ge), mid, hi)
        c_hi = jnp.where(upd & (~ge), c, c_hi)
        done = jnp.where(adjacent | (c_lo == kf), 1.0, done)
        return lo, hi, c_lo, c_hi, done

    def search_body(st):
        it, rest = st[0], st[1:]
        for _ in range(DSA_PROBES_PER_CHECK):
            rest = probe(rest)
        return (it + DSA_PROBES_PER_CHECK,) + tuple(rest)

    _, lo, hi, c_lo, c_hi, _ = lax.while_loop(search_cond, search_body, (jnp.int32(0), lo, hi, c_lo, c_hi, done))

    nl = ck // LANES
    lo_w = _tile_lanes(lo, nl)

    def write_plain():
        def body(j, _):
            idx_scr[j] = jnp.where(idx_scr[j] >= lo_w, 0.0, MASKED)
            return 0
        lax.fori_loop(0, nch, body, 0)

    def write_ties():
        hi_w = _tile_lanes(hi, nl)
        need = _tile_lanes(kf - c_hi, nl)
        urow = lax.broadcasted_iota(jnp.int32, (ck, ck), 0)
        ucol = lax.broadcasted_iota(jnp.int32, (ck, ck), 1)
        upper = jnp.where(urow < ucol, 1.0, 0.0).astype(BF16)

        def body(j, before):
            x = idx_scr[j]
            above = x >= hi_w
            eq = jnp.where((x >= lo_w) & (~above), 1.0, 0.0)
            prefix = jnp.dot(eq.astype(BF16), upper, preferred_element_type=F32) + _tile_lanes(before, nl)
            sel = above | ((eq > 0.5) & (prefix < need))
            idx_scr[j] = jnp.where(sel, 0.0, MASKED)
            tot = eq[:, :LANES]
            for c in range(1, nl):
                tot = tot + eq[:, c * LANES:(c + 1) * LANES]
            return before + _row_total(tot)

        lax.fori_loop(0, nch, body, jnp.zeros((bq, LANES), F32))

    lax.cond(jnp.max(c_lo) > kf, write_ties, write_plain)

    grp = DSA_GROUP * bq
    ngrp = rows // grp

    def att_step(j, carry):
        off = pl.multiple_of(j * ck, ck)
        k = kc_scr[pl.ds(off, ck), :]
        v = vc_scr[pl.ds(off, ck), :]
        bias = jnp.concatenate([idx_scr[j]] * DSA_GROUP, axis=0)
        out = []
        for g in range(ngrp):
            m, acc = carry[g]
            s = lax.dot_general(q_scr[g * grp:(g + 1) * grp, :], k, _NT, preferred_element_type=F32)
            parts = [s[:, c * LANES:(c + 1) * LANES] + bias[:, c * LANES:(c + 1) * LANES] for c in range(nl)]
            mc = parts[0]
            for c in range(1, nl):
                mc = jnp.maximum(mc, parts[c])
            m_new = jnp.maximum(m, jnp.max(mc, axis=1, keepdims=True))
            alpha = jnp.exp(m - m_new)
            p = jnp.concatenate([jnp.exp(x - m_new) for x in parts], axis=1).astype(BF16)
            acc = _tile_lanes(alpha, 2) * acc + jnp.dot(p, v, preferred_element_type=F32)
            out.append((m_new, acc))
        return tuple(out)

    init = tuple((jnp.full((grp, LANES), MASKED, F32), jnp.zeros((grp, 2 * LANES), F32)) for _ in range(ngrp))
    res = lax.fori_loop(0, nch, att_step, init)
    for g in range(ngrp):
        _, acc = res[g]
        o = acc[:, :LANES] / acc[:, LANES:]
        for t in range(DSA_GROUP):
            hl = slice((g * DSA_GROUP + t) * LANES, (g * DSA_GROUP + t + 1) * LANES)
            o_ref[:, hl] = (o[t * bq:(t + 1) * bq] * _silu(gate_ref[:, hl].astype(F32))).astype(o_ref.dtype)


def _dsa_attention(proj, small, dsa_tabs_q, dsa_tabs_k, idx_tabs, batch, seq):
    t = batch * seq
    bq = DSA_BQ
    nq = seq // bq
    topk = min(TOPK_MAX, seq // 4)
    rows = DSA_HEADS * bq
    idx_w = IDX_HEADS * IDX_DIM
    c_dsa_q = 2 * DSA_WIDTH + 3 * FOX_WIDTH
    qrow = lambda b, i: (i, 0)
    full = lambda b, i: (0, 0)
    return pl.pallas_call(
        functools.partial(_dsa_kernel, seq=seq, topk=topk),
        grid=(batch, nq),
        in_specs=[pl.BlockSpec((bq, DSA_WIDTH), lambda b, i: (b * nq + i, c_dsa_q // DSA_WIDTH)),
                  pl.BlockSpec((bq, idx_w), lambda b, i: (b * nq + i, (c_dsa_q + DSA_WIDTH) // idx_w)),
                  pl.BlockSpec((bq, 3 * LANES), lambda b, i: (b * nq + i, 0)),
                  pl.BlockSpec((seq, 3 * LANES), lambda b, i: (b, 0)),
                  pl.BlockSpec((bq, DSA_WIDTH), lambda b, i: (b * nq + i, 0)),
                  pl.BlockSpec((bq, LANES), qrow), pl.BlockSpec((bq, LANES), qrow),
                  pl.BlockSpec((seq, LANES), full), pl.BlockSpec((seq, LANES), full),
                  pl.BlockSpec((bq, LANES), qrow), pl.BlockSpec((bq, LANES), qrow),
                  pl.BlockSpec((seq, LANES), full), pl.BlockSpec((seq, LANES), full)],
        out_specs=pl.BlockSpec((bq, DSA_WIDTH), lambda b, i: (b * nq + i, 0)),
        out_shape=jax.ShapeDtypeStruct((t, DSA_WIDTH), BF16),
        scratch_shapes=[pltpu.VMEM((seq, LANES), BF16), pltpu.VMEM((seq, 2 * LANES), BF16),
                        pltpu.VMEM((seq, LANES), BF16), pltpu.VMEM((seq // DSA_CK, bq, DSA_CK), F32),
                        pltpu.VMEM((rows, LANES), BF16), pltpu.VMEM((rows, LANES), BF16),
                        pltpu.VMEM((rows, LANES), F32)],
        compiler_params=_params("arbitrary", "arbitrary"),
        name="dsa_attention",
    )(proj, proj, small, small, proj, *dsa_tabs_q, *dsa_tabs_k, *idx_tabs, *idx_tabs)


def _split(w, sizes):
    out, acc = [], 0
    for s in sizes:
        out.append(w[:, acc:acc + s])
        acc += s
    return out


def _bf16_cols(parts):
    return jnp.concatenate([p.astype(BF16) for p in parts], axis=1)


REGROUP_W = 2 * LANES
REGROUP_ROWS = 512


def _regroup_kernel(a_ref, b_ref, o_ref, *, segs):
    c = pl.program_id(0)
    kdim = a_ref.shape[0]
    tr = min(REGROUP_ROWS, kdim)
    lane = lax.broadcasted_iota(jnp.int32, (tr, LANES), 1)
    for c0, c1, r, _, _ in segs:
        @pl.when(jnp.logical_and(c >= c0, c < c1))
        def _(r=r):
            def body(t, _):
                rows = pl.ds(pl.multiple_of(t * tr, tr), tr)
                if r == 0:
                    o_ref[rows, :] = a_ref[rows, :].astype(BF16)
                else:
                    groups = [a_ref[rows, :LANES], a_ref[rows, LANES:], b_ref[rows, :]]
                    rolled = [pltpu.roll(g, LANES - r, axis=1) for g in groups]
                    for u in range(2):
                        o_ref[rows, u * LANES:(u + 1) * LANES] = jnp.where(
                            lane < LANES - r, rolled[u], rolled[u + 1]).astype(BF16)
                return 0

            lax.fori_loop(0, kdim // tr, body, 0)


def _regroup_cols(w, segments, name):
    kdim, n = w.shape
    segs, c = [], 0
    for first, width in segments:
        r = first % LANES
        base = first - r
        assert base % REGROUP_W == 0 and width % REGROUP_W == 0, (first, width)
        nb = width // REGROUP_W
        segs.append((c, c + nb, r, base // REGROUP_W, base // LANES + REGROUP_W // LANES))
        c += nb
    last_b = (n - 1) // LANES

    def a_map(cc):
        idx = 0
        for c0, c1, _, a0, _ in segs:
            idx = jnp.where(jnp.logical_and(cc >= c0, cc < c1), a0 + cc - c0, idx)
        return 0, idx

    def b_map(cc):
        idx = 0
        for c0, c1, _, _, b0 in segs:
            idx = jnp.where(jnp.logical_and(cc >= c0, cc < c1), b0 + (REGROUP_W // LANES) * (cc - c0), idx)
        return 0, jnp.minimum(idx, last_b)

    return pl.pallas_call(
        functools.partial(_regroup_kernel, segs=tuple(segs)),
        grid=(c,),
        in_specs=[pl.BlockSpec((kdim, REGROUP_W), a_map), pl.BlockSpec((kdim, LANES), b_map)],
        out_specs=pl.BlockSpec((kdim, REGROUP_W), lambda cc: (0, cc)),
        out_shape=jax.ShapeDtypeStruct((kdim, c * REGROUP_W), BF16),
        compiler_params=_params("arbitrary"),
        name=name,
    )(w, w)


def _even_layer(x2, batch, seq, norm_g, w_in, q_norm_g, w_uq, kv_norm_g, w_ukv, diff_lambda, subln_g, w_out,
                lambda_init, bq):
    d = x2.shape[1]
    n_lat = MLA_Q_RANK + MLA_KV_RANK + MLA_ROPE
    w_big = _regroup_cols(w_in, [(n_lat, 2 * MLA_WIDTH + 3 * DIFF_WIDTH)], "even_w_regroup")
    w_lat = jnp.pad(w_in[:, :n_lat].astype(BF16), ((0, 0), (0, LANES - MLA_ROPE)))
    uq = w_uq.astype(BF16).reshape(MLA_Q_RANK, MLA_HEADS, MLA_NOPE + MLA_ROPE)
    uq = jnp.concatenate([uq, jnp.zeros((MLA_Q_RANK, MLA_HEADS, 2 * LANES - MLA_NOPE - MLA_ROPE), BF16)], axis=2)
    uq = uq.reshape(MLA_Q_RANK, MLA_HEADS * 2 * LANES)
    ukv = w_ukv.astype(BF16).reshape(MLA_KV_RANK, MLA_HEADS, MLA_NOPE + MLA_V)
    ukv = jnp.concatenate([ukv[:, :, :MLA_NOPE].reshape(MLA_KV_RANK, -1), ukv[:, :, MLA_NOPE:].reshape(MLA_KV_RANK, -1)],
                          axis=1)

    h = _rmsnorm(x2, norm_g, BF16)
    proj = _matmul(h, w_big, BF16, tm=1024, tn=1024, name="even_inproj")
    lat = _matmul(h, w_lat, BF16, tm=512, tn=w_lat.shape[1], name="even_latent")
    q = _norm_matmul(lat, 0, q_norm_g, uq, tm=1024, tn=1024, name="mla_q_up")
    kv = _norm_matmul(lat, MLA_Q_RANK // MLA_KV_RANK, kv_norm_g, ukv, tm=1024, tn=1024, name="mla_kv_up")

    mla_scale = (MLA_NOPE + MLA_ROPE) ** -0.5
    o_a = _mla_attention(q, kv, lat, proj, _rope_tables(seq, MLA_ROPE, LANES, mla_scale),
                         _rope_tables(seq, MLA_ROPE, LANES), batch, seq, bq=bq)
    rot = DIFF_DIM // ROPE_FRACTION
    o_b = _diff_attention(proj, diff_lambda, subln_g, _rope_tables(seq, rot, LANES, DIFF_DIM ** -0.5),
                          _rope_tables(seq, rot, LANES), batch, seq, lambda_init, bq=bq)
    return _outproj(o_a, o_b, w_out.astype(BF16), x2, tm=1024, tn=1024)


def _odd_layer(x2, batch, seq, norm_g, w_in, forget_bias, w_out, bq):
    d = x2.shape[1]
    (dsa_q, dsa_k, dsa_v, idx_q, idx_k, idx_w, gate_c, fox_q, fox_k, fox_v, fox_f, gate_d) = _split(
        w_in, (DSA_WIDTH, DSA_DIM, DSA_DIM, IDX_HEADS * IDX_DIM, IDX_DIM, IDX_HEADS, DSA_WIDTH,
               FOX_WIDTH, FOX_WIDTH, FOX_WIDTH, FOX_HEADS, FOX_WIDTH))
    c_idx_q = DSA_WIDTH + 2 * DSA_DIM
    c_gate_c = c_idx_q + IDX_HEADS * IDX_DIM + IDX_DIM + IDX_HEADS
    c_gate_d = c_gate_c + DSA_WIDTH + 3 * FOX_WIDTH + FOX_HEADS
    w_big = _regroup_cols(w_in, [(c_gate_c, DSA_WIDTH + 3 * FOX_WIDTH), (c_gate_d, FOX_WIDTH), (0, DSA_WIDTH),
                                 (c_idx_q, IDX_HEADS * IDX_DIM)], "odd_w_regroup")
    pad = 3 * LANES - (2 * DSA_DIM + IDX_DIM + IDX_HEADS + FOX_HEADS)
    w_small = _bf16_cols([dsa_k, dsa_v, idx_k, idx_w, fox_f, jnp.zeros((d, pad), w_in.dtype)])

    h = _rmsnorm(x2, norm_g, BF16)
    proj = _matmul(h, w_big, BF16, tm=1024, tn=1024, name="odd_inproj")
    small = _matmul(h, w_small, F32, tm=1024, tn=3 * LANES, name="odd_small")

    rot = DSA_DIM // ROPE_FRACTION
    o_c = _dsa_attention(proj, small, _rope_tables(seq, rot, LANES, DSA_DIM ** -0.5), _rope_tables(seq, rot, LANES),
                         _rope_tables(seq, IDX_DIM // ROPE_FRACTION, IDX_DIM), batch, seq)
    cum = _logf_cumsum(small, forget_bias, batch, seq)
    cum_t = cum[:, FOX_F_LANE:FOX_F_LANE + FOX_HEADS].reshape(batch, seq, FOX_HEADS).transpose(0, 2, 1)
    cum_t = cum_t.reshape(batch * FOX_HEADS, seq // FLASH_SUB, FLASH_SUB)
    o_d = _fox_attention(proj, cum, cum_t, batch, seq, bq=bq)
    return _outproj(o_c, o_d, w_out.astype(BF16), x2, tm=1024, tn=1024)


def kernel(x, even_norm, even_w_in, mla_q_norm, mla_w_uq, mla_kv_norm, mla_w_ukv, diff_lambda, diff_subln, even_w_out, odd_norm, odd_w_in, fox_forget_bias, odd_w_out, final_norm):
    batch, seq, d = x.shape
    bq = min(1024, seq)
    h = x.reshape(batch * seq, d)
    depth = even_norm.shape[0] + odd_norm.shape[0]
    for layer in range(depth):
        i = layer // 2
        if layer % 2 == 0:
            lambda_init = 0.8 - 0.6 * math.exp(-0.3 * layer)
            h = _even_layer(h, batch, seq, even_norm[i], even_w_in[i], mla_q_norm[i], mla_w_uq[i], mla_kv_norm[i],
                            mla_w_ukv[i], diff_lambda[i], diff_subln[i], even_w_out[i], lambda_init, bq)
        else:
            h = _odd_layer(h, batch, seq, odd_norm[i], odd_w_in[i], fox_forget_bias[i], odd_w_out[i], bq)
    return _rmsnorm(h, final_norm, x.dtype).reshape(batch, seq, d)
```

```python
import functools
import math

import jax
import jax.numpy as jnp
from jax import lax
from jax.experimental import pallas as pl
from jax.experimental.pallas import tpu as pltpu

F32 = jnp.float32
BF16 = jnp.bfloat16

ROPE_THETA = 500000.0
NORM_EPS = 1e-6
ROPE_FRACTION = 4

MLA_HEADS, MLA_NOPE, MLA_ROPE, MLA_V = 16, 128, 64, 128
MLA_Q_RANK, MLA_KV_RANK = 1024, 512
DIFF_HEADS, DIFF_DIM = 8, 128
DSA_HEADS, DSA_DIM = 16, 128
IDX_HEADS, IDX_DIM = 16, 64
TOPK_MAX = 256
FOX_HEADS, FOX_DIM = 16, 128

MLA_WIDTH = MLA_HEADS * MLA_V
DIFF_WIDTH = DIFF_HEADS * 2 * DIFF_DIM
DSA_WIDTH = DSA_HEADS * DSA_DIM
FOX_WIDTH = FOX_HEADS * FOX_DIM

LANES = 128
VMEM_LIMIT_BYTES = 56 * 1024 * 1024
MASKED = -1e30

IDX_W_LANE = IDX_DIM
FOX_F_LANE = IDX_DIM + IDX_HEADS

_NT = (((1,), (1,)), ((), ()))


def _params(*sem):
    return pltpu.CompilerParams(dimension_semantics=sem, vmem_limit_bytes=VMEM_LIMIT_BYTES)


def _rmsnorm_kernel(x_ref, g_ref, o_ref):
    x = x_ref[...].astype(F32)
    y = x * lax.rsqrt(jnp.mean(x * x, axis=-1, keepdims=True) + NORM_EPS)
    o_ref[...] = (y * g_ref[...]).astype(o_ref.dtype)


def _rmsnorm(x, g, out_dtype, tm=256):
    t, d = x.shape
    return pl.pallas_call(
        _rmsnorm_kernel,
        grid=(t // tm,),
        in_specs=[pl.BlockSpec((tm, d), lambda i: (i, 0)), pl.BlockSpec((1, d), lambda i: (0, 0))],
        out_specs=pl.BlockSpec((tm, d), lambda i: (i, 0)),
        out_shape=jax.ShapeDtypeStruct((t, d), out_dtype),
        compiler_params=_params("arbitrary"),
        name="rmsnorm",
    )(x, g.reshape(1, d).astype(F32))


def _mm_nt_kernel(a_ref, bt_ref, o_ref):
    bt = bt_ref[...].astype(BF16)
    o_ref[...] = lax.dot_general(a_ref[...], bt, _NT, preferred_element_type=F32).astype(o_ref.dtype)


SUBLANES = 8


def _matmul_nt(a, bt, row_ranges, out_dtype, *, tm, tn, name):
    m, kdim = a.shape
    starts = []
    for first, n_rows in row_ranges:
        assert first % SUBLANES == 0 and n_rows % tn == 0, (first, n_rows)
        starts += [first + t * tn for t in range(n_rows // tn)]
    n = len(starts) * tn

    def bt_map(j, i):
        q = starts[-1] // SUBLANES
        for t in range(len(starts) - 2, -1, -1):
            q = jnp.where(j <= t, starts[t] // SUBLANES, q)
        return q * SUBLANES, 0

    return pl.pallas_call(
        _mm_nt_kernel,
        grid=(len(starts), m // tm),
        in_specs=[pl.BlockSpec((tm, kdim), lambda j, i: (i, 0)),
                  pl.BlockSpec((pl.Element(tn), pl.Element(kdim)), bt_map)],
        out_specs=pl.BlockSpec((tm, tn), lambda j, i: (i, j)),
        out_shape=jax.ShapeDtypeStruct((m, n), out_dtype),
        compiler_params=_params("arbitrary", "arbitrary"),
        name=name,
    )(a, bt)


def _norm_mm_kernel(a_ref, g_ref, b_ref, o_ref):
    x = a_ref[...].astype(F32)
    y = x * lax.rsqrt(jnp.mean(x * x, axis=-1, keepdims=True) + NORM_EPS) * g_ref[...]
    o_ref[...] = jnp.dot(y.astype(BF16), b_ref[...], preferred_element_type=F32).astype(o_ref.dtype)


def _norm_matmul(a, a_col_block, g, b, *, tm, tn, name):
    m = a.shape[0]
    kdim, n = b.shape
    return pl.pallas_call(
        _norm_mm_kernel,
        grid=(m // tm, n // tn),
        in_specs=[pl.BlockSpec((tm, kdim), lambda i, j: (i, a_col_block)),
                  pl.BlockSpec((1, kdim), lambda i, j: (0, 0)),
                  pl.BlockSpec((kdim, tn), lambda i, j: (0, j))],
        out_specs=pl.BlockSpec((tm, tn), lambda i, j: (i, j)),
        out_shape=jax.ShapeDtypeStruct((m, n), BF16),
        compiler_params=_params("arbitrary", "arbitrary"),
        name=name,
    )(a, g.reshape(1, kdim).astype(F32), b)


def _outproj_kernel(oa_ref, ob_ref, w_ref, res_ref, h_ref):
    ka = oa_ref.shape[1]
    acc = jnp.dot(oa_ref[...], w_ref[:ka, :], preferred_element_type=F32)
    acc = acc + jnp.dot(ob_ref[...], w_ref[ka:, :], preferred_element_type=F32)
    h_ref[...] = acc + res_ref[...]


def _outproj(oa, ob, w, res, *, tm, tn):
    m, ka = oa.shape
    kb = ob.shape[1]
    n = w.shape[1]
    return pl.pallas_call(
        _outproj_kernel,
        grid=(m // tm, n // tn),
        in_specs=[pl.BlockSpec((tm, ka), lambda i, j: (i, 0)),
                  pl.BlockSpec((tm, kb), lambda i, j: (i, 0)),
                  pl.BlockSpec((ka + kb, tn), lambda i, j: (0, j)),
                  pl.BlockSpec((tm, tn), lambda i, j: (i, j))],
        out_specs=pl.BlockSpec((tm, tn), lambda i, j: (i, j)),
        out_shape=jax.ShapeDtypeStruct((m, n), F32),
        compiler_params=_params("arbitrary", "arbitrary"),
        name="outproj",
    )(oa, ob, w, res)


def _rope_tables(seq, rot_dim, period, scale=1.0):
    inv = ROPE_THETA ** (-jnp.arange(0, rot_dim, 2, dtype=F32) / rot_dim)
    ang = jnp.arange(seq, dtype=F32)[:, None] * inv[None, :]
    cos, sin = jnp.cos(ang), jnp.sin(ang)
    ones = jnp.ones((seq, period - rot_dim), F32)
    c = jnp.concatenate([cos, cos, ones], axis=1) * scale
    s = jnp.concatenate([-sin, sin, 0.0 * ones], axis=1) * scale
    reps = LANES // period
    return jnp.tile(c, (1, reps)), jnp.tile(s, (1, reps))


def _rope(x, c, s, half, period):
    lane = lax.broadcasted_iota(jnp.int32, x.shape, 1)
    first = (lane & (period - 1)) < half
    partner = jnp.where(first, pltpu.roll(x, LANES - half, axis=1), pltpu.roll(x, half, axis=1))
    return x * c + partner * s


def _softmax_pv(s, v, carry, mask=None):
    m, l, acc = carry
    if mask is not None:
        s = jnp.where(mask, s, MASKED)
    m_new = jnp.maximum(m, jnp.max(s, axis=1, keepdims=True))
    alpha = jnp.exp(m - m_new)
    p = jnp.exp(s - m_new)
    l = alpha * l + jnp.sum(p, axis=1, keepdims=True)
    acc = alpha * acc + jnp.dot(p.astype(BF16), v, preferred_element_type=F32)
    return m_new, l, acc


def _init_carry(rows, dv):
    return (jnp.full((rows, 1), MASKED, F32), jnp.zeros((rows, 1), F32), jnp.zeros((rows, dv), F32))


FLASH_SUB = 512


def _tri_mask(n):
    row = lax.broadcasted_iota(jnp.int32, (n, n), 0)
    col = lax.broadcasted_iota(jnp.int32, (n, n), 1)
    return row >= col


def _causal_sweep(logits, values, i, bq, n_chain, chains_per_sub, dv):
    sub = FLASH_SUB
    nsub = bq // sub

    def step(j, st):
        v = values(j * nsub, nsub)
        return tuple(_softmax_pv(logits(t, j * nsub, nsub), v, st[t]) for t in range(n_chain))

    st = list(lax.fori_loop(0, i, step, tuple(_init_carry(sub, dv) for _ in range(n_chain))))
    mask = _tri_mask(sub)
    for c in range(nsub):
        blk = i * nsub + c
        v = values(blk, 1)
        for t in range(n_chain):
            r = t // chains_per_sub
            if r >= c:
                st[t] = _softmax_pv(logits(t, blk, 1), v, st[t], mask if r == c else None)
    return st


def _key_rows(blk, nb):
    return pl.ds(pl.multiple_of(blk * FLASH_SUB, FLASH_SUB), nb * FLASH_SUB)


def _silu(g):
    return g * jax.nn.sigmoid(g)


def _mla_kernel(q_ref, kn_ref, kr_ref, v_ref, gate_ref, cq_ref, sq_ref, ck_ref, sk_ref, o_ref, k_scr, *, bq, scale):
    i = pl.program_id(2)
    half = MLA_ROPE // 2
    sub = FLASH_SUB

    @pl.when(i == 0)
    def _():
        k_scr[:, :LANES] = kn_ref[...]
        k_scr[:, LANES:] = _rope(kr_ref[...].astype(F32), ck_ref[...], sk_ref[...], half, LANES).astype(BF16)

    qs = []
    for r in range(bq // sub):
        rs = slice(r * sub, (r + 1) * sub)
        qn = (q_ref[rs, :LANES].astype(F32) * scale).astype(BF16)
        qr = _rope(q_ref[rs, LANES:].astype(F32), cq_ref[rs, :], sq_ref[rs, :], half, LANES).astype(BF16)
        qs.append(jnp.concatenate([qn, qr], axis=1))

    def logits(t, blk, nb):
        return lax.dot_general(qs[t], k_scr[_key_rows(blk, nb), :], _NT, preferred_element_type=F32)

    def values(blk, nb):
        return v_ref[_key_rows(blk, nb), :]

    st = _causal_sweep(logits, values, i, bq, bq // sub, 1, MLA_V)
    for r, (_, l, acc) in enumerate(st):
        rs = slice(r * sub, (r + 1) * sub)
        o_ref[rs, :] = (acc / l * _silu(gate_ref[rs, :].astype(F32))).astype(o_ref.dtype)


def _mla_attention(q, kv, lat, proj, tabs_q, tabs_k, batch, seq, *, bq):
    t = batch * seq
    nq = seq // bq
    kr_block = (MLA_Q_RANK + MLA_KV_RANK) // LANES
    qrow = lambda b, h, i: (i, 0)
    full = lambda b, h, i: (0, 0)
    return pl.pallas_call(
        functools.partial(_mla_kernel, bq=bq, scale=(MLA_NOPE + MLA_ROPE) ** -0.5),
        grid=(batch, MLA_HEADS, nq),
        in_specs=[pl.BlockSpec((bq, 2 * LANES), lambda b, h, i: (b * nq + i, h)),
                  pl.BlockSpec((seq, LANES), lambda b, h, i: (b, h)),
                  pl.BlockSpec((seq, LANES), lambda b, h, i: (b, kr_block)),
                  pl.BlockSpec((seq, LANES), lambda b, h, i: (b, MLA_HEADS + h)),
                  pl.BlockSpec((bq, MLA_V), lambda b, h, i: (b * nq + i, h)),
                  pl.BlockSpec((bq, LANES), qrow), pl.BlockSpec((bq, LANES), qrow),
                  pl.BlockSpec((seq, LANES), full), pl.BlockSpec((seq, LANES), full)],
        out_specs=pl.BlockSpec((bq, MLA_V), lambda b, h, i: (b * nq + i, h)),
        out_shape=jax.ShapeDtypeStruct((t, MLA_WIDTH), BF16),
        scratch_shapes=[pltpu.VMEM((seq, 2 * LANES), BF16)],
        compiler_params=_params("arbitrary", "arbitrary", "arbitrary"),
        name="mla_attention",
    )(q, kv, lat, kv, proj, *tabs_q, *tabs_k)


def _diff_kernel(q_ref, k_ref, v_ref, gate_ref, lam_ref, g_ref, cq_ref, sq_ref, ck_ref, sk_ref, o_ref, k_scr,
                 *, bq, lambda_init):
    i = pl.program_id(2)
    half = DIFF_DIM // ROPE_FRACTION // 2
    sub = FLASH_SUB

    @pl.when(i == 0)
    def _():
        for c in range(2):
            sl = slice(c * LANES, (c + 1) * LANES)
            k_scr[:, sl] = _rope(k_ref[:, sl].astype(F32), ck_ref[...], sk_ref[...], half, LANES).astype(BF16)

    qs = []
    for r in range(bq // sub):
        rs = slice(r * sub, (r + 1) * sub)
        for c in range(2):
            x = q_ref[rs, c * LANES:(c + 1) * LANES].astype(F32)
            qs.append(_rope(x, cq_ref[rs, :], sq_ref[rs, :], half, LANES).astype(BF16))

    def logits(t, blk, nb):
        c = t % 2
        return lax.dot_general(qs[t], k_scr[_key_rows(blk, nb), c * LANES:(c + 1) * LANES], _NT,
                               preferred_element_type=F32)

    def values(blk, nb):
        return v_ref[_key_rows(blk, nb), :]

    st = _causal_sweep(logits, values, i, bq, 2 * (bq // sub), 2, 2 * DIFF_DIM)

    lp = lam_ref[...]
    lam = (jnp.exp(jnp.sum(lp[0:1] * lp[1:2], axis=1, keepdims=True))
           - jnp.exp(jnp.sum(lp[2:3] * lp[3:4], axis=1, keepdims=True)) + lambda_init)
    for r in range(bq // sub):
        rs = slice(r * sub, (r + 1) * sub)
        (_, l0, acc0), (_, l1, acc1) = st[2 * r], st[2 * r + 1]
        o = acc0 / l0 - lam * (acc1 / l1)
        o = o * lax.rsqrt(jnp.mean(o * o, axis=-1, keepdims=True) + NORM_EPS) * g_ref[...]
        o_ref[rs, :] = (o * (1.0 - lambda_init) * _silu(gate_ref[rs, :].astype(F32))).astype(o_ref.dtype)


def _diff_attention(proj, lam_params, subln_g, tabs_q, tabs_k, batch, seq, lambda_init, *, bq):
    t = batch * seq
    nq = seq // bq
    w = 2 * DIFF_DIM
    q0 = MLA_WIDTH // w
    k0, v0, g0 = q0 + DIFF_HEADS, q0 + 2 * DIFF_HEADS, q0 + 3 * DIFF_HEADS
    qrow = lambda b, h, i: (i, 0)
    full = lambda b, h, i: (0, 0)
    return pl.pallas_call(
        functools.partial(_diff_kernel, bq=bq, lambda_init=lambda_init),
        grid=(batch, DIFF_HEADS, nq),
        in_specs=[pl.BlockSpec((bq, w), lambda b, h, i: (b * nq + i, q0 + h)),
                  pl.BlockSpec((seq, w), lambda b, h, i: (b, k0 + h)),
                  pl.BlockSpec((seq, w), lambda b, h, i: (b, v0 + h)),
                  pl.BlockSpec((bq, w), lambda b, h, i: (b * nq + i, g0 + h)),
                  pl.BlockSpec((4, DIFF_DIM), full),
                  pl.BlockSpec((1, w), full),
                  pl.BlockSpec((bq, LANES), qrow), pl.BlockSpec((bq, LANES), qrow),
                  pl.BlockSpec((seq, LANES), full), pl.BlockSpec((seq, LANES), full)],
        out_specs=pl.BlockSpec((bq, w), lambda b, h, i: (b * nq + i, h)),
        out_shape=jax.ShapeDtypeStruct((t, DIFF_WIDTH), BF16),
        scratch_shapes=[pltpu.VMEM((seq, w), BF16)],
        compiler_params=_params("arbitrary", "arbitrary", "arbitrary"),
        name="diff_attention",
    )(proj, proj, proj, proj, lam_params.astype(F32), subln_g.reshape(1, w).astype(F32), *tabs_q, *tabs_k)


def _fox_kernel(q_ref, k_ref, v_ref, gate_ref, cum_ref, cumt_ref, o_ref, *, bq, scale):
    h = pl.program_id(1)
    i = pl.program_id(2)
    sub = FLASH_SUB
    lane = lax.broadcasted_iota(jnp.int32, (sub, LANES), 1)
    qs, cqs = [], []
    for r in range(bq // sub):
        rs = slice(r * sub, (r + 1) * sub)
        qs.append((q_ref[rs, :].astype(F32) * scale).astype(BF16))
        cqs.append(jnp.sum(jnp.where(lane == FOX_F_LANE + h, cum_ref[rs, :], 0.0), axis=1, keepdims=True))

    def logits(t, blk, nb):
        s = lax.dot_general(qs[t], k_ref[_key_rows(blk, nb), :], _NT, preferred_element_type=F32)
        ck = jnp.concatenate([cumt_ref[0, pl.ds(blk + n, 1), :] for n in range(nb)], axis=1)
        return s + (cqs[t] - ck)

    def values(blk, nb):
        return v_ref[_key_rows(blk, nb), :]

    st = _causal_sweep(logits, values, i, bq, bq // sub, 1, FOX_DIM)
    for r, (_, l, acc) in enumerate(st):
        rs = slice(r * sub, (r + 1) * sub)
        o_ref[rs, :] = (acc / l * _silu(gate_ref[rs, :].astype(F32))).astype(o_ref.dtype)


def _fox_attention(proj, cum, cum_t, batch, seq, *, bq):
    t = batch * seq
    nq = seq // bq
    q0 = DSA_WIDTH // LANES
    k0, v0, g0 = q0 + FOX_HEADS, q0 + 2 * FOX_HEADS, q0 + 3 * FOX_HEADS
    return pl.pallas_call(
        functools.partial(_fox_kernel, bq=bq, scale=FOX_DIM ** -0.5),
        grid=(batch, FOX_HEADS, nq),
        in_specs=[pl.BlockSpec((bq, LANES), lambda b, h, i: (b * nq + i, q0 + h)),
                  pl.BlockSpec((seq, LANES), lambda b, h, i: (b, k0 + h)),
                  pl.BlockSpec((seq, LANES), lambda b, h, i: (b, v0 + h)),
                  pl.BlockSpec((bq, LANES), lambda b, h, i: (b * nq + i, g0 + h)),
                  pl.BlockSpec((bq, LANES), lambda b, h, i: (b * nq + i, 0)),
                  pl.BlockSpec((1, seq // FLASH_SUB, FLASH_SUB), lambda b, h, i: (b * FOX_HEADS + h, 0, 0))],
        out_specs=pl.BlockSpec((bq, FOX_DIM), lambda b, h, i: (b * nq + i, h)),
        out_shape=jax.ShapeDtypeStruct((t, FOX_WIDTH), BF16),
        compiler_params=_params("arbitrary", "arbitrary", "arbitrary"),
        name="fox_attention",
    )(proj, proj, proj, proj, cum, cum_t)


def _logf_cumsum_kernel(f_ref, b_ref, o_ref, carry_ref, *, tb):
    @pl.when(pl.program_id(1) == 0)
    def _():
        carry_ref[...] = jnp.zeros_like(carry_ref)

    x = f_ref[...] + b_ref[...]
    logf = jnp.minimum(x, 0.0) - jnp.log1p(jnp.exp(-jnp.abs(x)))
    row = lax.broadcasted_iota(jnp.int32, (tb, tb), 0)
    col = lax.broadcasted_iota(jnp.int32, (tb, tb), 1)
    tri = jnp.where(row >= col, 1.0, 0.0).astype(BF16)
    hi = logf.astype(BF16)
    r1 = logf - hi.astype(F32)
    mid = r1.astype(BF16)
    lo = (r1 - mid.astype(F32)).astype(BF16)
    cum = (jnp.dot(tri, hi, preferred_element_type=F32) + jnp.dot(tri, mid, preferred_element_type=F32)
           + jnp.dot(tri, lo, preferred_element_type=F32)) + carry_ref[...]
    o_ref[...] = cum
    carry_ref[...] = cum[tb - 1:tb, :]


def _logf_cumsum(small, forget_bias, batch, seq, *, tb=512):
    t = batch * seq
    nb = seq // tb
    bias = jnp.zeros((1, LANES), F32).at[0, FOX_F_LANE:FOX_F_LANE + FOX_HEADS].set(forget_bias.astype(F32))
    return pl.pallas_call(
        functools.partial(_logf_cumsum_kernel, tb=tb),
        grid=(batch, nb),
        in_specs=[pl.BlockSpec((tb, LANES), lambda b, i: (b * nb + i, 2)),
                  pl.BlockSpec((1, LANES), lambda b, i: (0, 0))],
        out_specs=pl.BlockSpec((tb, LANES), lambda b, i: (b * nb + i, 0)),
        out_shape=jax.ShapeDtypeStruct((t, LANES), F32),
        scratch_shapes=[pltpu.VMEM((1, LANES), F32)],
        compiler_params=_params("arbitrary", "arbitrary"),
        name="logf_cumsum",
    )(small, bias)


DSA_CK = 512
DSA_BQ = 128
DSA_GROUP = 4
DSA_MAX_ITERS = 640
DSA_PROBES_PER_CHECK = 4


def _row_total(x):
    return jnp.broadcast_to(jnp.sum(x, axis=1, keepdims=True), x.shape)


def _tile_lanes(x, n):
    return jnp.concatenate([x] * n, axis=1)


def _dsa_kernel(qc_ref, qi_ref, wq_ref, kv_ref, gate_ref, cdq_ref, sdq_ref, cdk_ref, sdk_ref, ciq_ref, siq_ref,
                cik_ref, sik_ref, o_ref, kc_scr, vc_scr, ki_scr, idx_scr, q_scr, qi_scr, w_scr, *, seq, topk):
    i = pl.program_id(1)
    bq, ck = DSA_BQ, DSA_CK
    dhalf = DSA_DIM // ROPE_FRACTION // 2
    ihalf = IDX_DIM // ROPE_FRACTION // 2
    rows = DSA_HEADS * bq
    lane = lax.broadcasted_iota(jnp.int32, (bq, LANES), 1)

    @pl.when(i == 0)
    def _():
        tb = 512

        def prep(r, _):
            off = pl.multiple_of(r * tb, tb)
            sl = pl.ds(off, tb)
            kc_scr[sl, :] = _rope(kv_ref[sl, 0:LANES], cdk_ref[sl, :], sdk_ref[sl, :], dhalf, LANES).astype(BF16)
            vc_scr[sl, :LANES] = kv_ref[sl, LANES:2 * LANES].astype(BF16)
            vc_scr[sl, LANES:] = jnp.ones((tb, LANES), BF16)
            lane_t = lax.broadcasted_iota(jnp.int32, (tb, LANES), 1)
            ki = jnp.where(lane_t < IDX_DIM, kv_ref[sl, 2 * LANES:3 * LANES], 0.0)
            ki = _rope(ki, cik_ref[sl, :], sik_ref[sl, :], ihalf, IDX_DIM)
            ki_scr[sl, :] = (ki + pltpu.roll(ki, IDX_DIM, axis=1)).astype(BF16)
            return 0

        lax.fori_loop(0, seq // tb, prep, 0)

    for h in range(DSA_HEADS):
        xh = qc_ref[:, h * LANES:(h + 1) * LANES].astype(F32)
        q_scr[h * bq:(h + 1) * bq, :] = _rope(xh, cdq_ref[...], sdq_ref[...], dhalf, LANES).astype(BF16)
    for p in range(IDX_HEADS // 2):
        xp = _rope(qi_ref[:, p * LANES:(p + 1) * LANES].astype(F32), ciq_ref[...], siq_ref[...], ihalf, IDX_DIM)
        qi_scr[(2 * p) * bq:(2 * p + 1) * bq, :] = jnp.where(lane < IDX_DIM, xp, 0.0).astype(BF16)
        qi_scr[(2 * p + 1) * bq:(2 * p + 2) * bq, :] = jnp.where(lane >= IDX_DIM, xp, 0.0).astype(BF16)
    w = wq_ref[:, 2 * LANES:3 * LANES] * ((IDX_HEADS ** -0.5) * (IDX_DIM ** -0.5))
    for h in range(IDX_HEADS):
        w_scr[h * bq:(h + 1) * bq, :] = _row_total(jnp.where(lane == IDX_W_LANE + h, w, 0.0))

    nch = i // (ck // bq) + 1
    qpos = i * bq + lax.broadcasted_iota(jnp.int32, (bq, LANES), 0)
    inf = jnp.full((bq, LANES), jnp.inf, F32)

    def idx_step(j, carry):
        mn, mx = carry
        off = pl.multiple_of(j * ck, ck)
        r = lax.dot_general(qi_scr[...], ki_scr[pl.ds(off, ck), :], _NT, preferred_element_type=F32)
        halves = []
        for c in range(ck // LANES):
            sc = None
            for h in range(IDX_HEADS):
                term = jnp.maximum(r[h * bq:(h + 1) * bq, c * LANES:(c + 1) * LANES], 0.0) * w_scr[h * bq:(h + 1) * bq, :]
                sc = term if sc is None else sc + term
            valid = (off + c * LANES + lane) <= qpos
            halves.append(jnp.where(valid, sc, -inf))
            mn = jnp.minimum(mn, jnp.where(valid, sc, inf))
            mx = jnp.maximum(mx, jnp.where(valid, sc, -inf))
        idx_scr[j] = jnp.concatenate(halves, axis=1)
        return mn, mx

    mn, mx = lax.fori_loop(0, nch, idx_step, (inf, -inf))
    mn = jnp.broadcast_to(jnp.min(mn, axis=1, keepdims=True), mn.shape)
    mx = jnp.broadcast_to(jnp.max(mx, axis=1, keepdims=True), mx.shape)

    def count_ge(t):
        def body(j, c):
            x = idx_scr[j]
            for half in range(ck // LANES):
                c = c + jnp.where(x[:, half * LANES:(half + 1) * LANES] >= t, 1.0, 0.0)
            return c
        return _row_total(lax.fori_loop(0, nch, body, jnp.zeros((bq, LANES), F32)))

    kf = float(topk)
    nvalid = (qpos + 1).astype(F32)
    c_mx = count_ge(mx)
    few = nvalid <= kf
    top_tie = (~few) & (c_mx >= kf)
    lo = jnp.where(top_tie, mx, mn)
    c_lo = jnp.where(top_tie, c_mx, nvalid)
    hi = jnp.where(top_tie, inf, mx)
    c_hi = jnp.where(top_tie, 0.0, c_mx)
    done = jnp.where(few | top_tie | (c_lo == kf), 1.0, 0.0)

    def search_cond(st):
        it, _, _, _, _, done = st
        return jnp.logical_and(it < DSA_MAX_ITERS, jnp.min(done) < 0.5)

    def probe(st):
        lo, hi, c_lo, c_hi, done = st
        mid = 0.5 * lo + 0.5 * hi
        adjacent = (mid <= lo) | (mid >= hi)
        c = count_ge(mid)
        ge = c >= kf
        upd = (done < 0.5) & (~adjacent)
        lo = jnp.where(upd & ge, mid, lo)
        c_lo = jnp.where(upd & ge, c, c_lo)
        hi = jnp.where(upd & (~ge), mid, hi)
        c_hi = jnp.where(upd & (~ge), c, c_hi)
        done = jnp.where(adjacent | (c_lo == kf), 1.0, done)
        return lo, hi, c_lo, c_hi, done

    def search_body(st):
        it, rest = st[0], st[1:]
        for _ in range(DSA_PROBES_PER_CHECK):
            rest = probe(rest)
        return (it + DSA_PROBES_PER_CHECK,) + tuple(rest)

    _, lo, hi, c_lo, c_hi, _ = lax.while_loop(search_cond, search_body, (jnp.int32(0), lo, hi, c_lo, c_hi, done))

    nl = ck // LANES
    lo_w = _tile_lanes(lo, nl)

    def write_plain():
        def body(j, _):
            idx_scr[j] = jnp.where(idx_scr[j] >= lo_w, 0.0, MASKED)
            return 0
        lax.fori_loop(0, nch, body, 0)

    def write_ties():
        hi_w = _tile_lanes(hi, nl)
        need = _tile_lanes(kf - c_hi, nl)
        urow = lax.broadcasted_iota(jnp.int32, (ck, ck), 0)
        ucol = lax.broadcasted_iota(jnp.int32, (ck, ck), 1)
        upper = jnp.where(urow < ucol, 1.0, 0.0).astype(BF16)

        def body(j, before):
            x = idx_scr[j]
            above = x >= hi_w
            eq = jnp.where((x >= lo_w) & (~above), 1.0, 0.0)
            prefix = jnp.dot(eq.astype(BF16), upper, preferred_element_type=F32) + _tile_lanes(before, nl)
            sel = above | ((eq > 0.5) & (prefix < need))
            idx_scr[j] = jnp.where(sel, 0.0, MASKED)
            tot = eq[:, :LANES]
            for c in range(1, nl):
                tot = tot + eq[:, c * LANES:(c + 1) * LANES]
            return before + _row_total(tot)

        lax.fori_loop(0, nch, body, jnp.zeros((bq, LANES), F32))

    lax.cond(jnp.max(c_lo) > kf, write_ties, write_plain)

    grp = DSA_GROUP * bq
    ngrp = rows // grp

    def att_step(j, carry):
        off = pl.multiple_of(j * ck, ck)
        k = kc_scr[pl.ds(off, ck), :]
        v = vc_scr[pl.ds(off, ck), :]
        bias = jnp.concatenate([idx_scr[j]] * DSA_GROUP, axis=0)
        out = []
        for g in range(ngrp):
            m, acc = carry[g]
            s = lax.dot_general(q_scr[g * grp:(g + 1) * grp, :], k, _NT, preferred_element_type=F32)
            parts = [s[:, c * LANES:(c + 1) * LANES] + bias[:, c * LANES:(c + 1) * LANES] for c in range(nl)]
            mc = parts[0]
            for c in range(1, nl):
                mc = jnp.maximum(mc, parts[c])
            m_new = jnp.maximum(m, jnp.max(mc, axis=1, keepdims=True))
            alpha = jnp.exp(m - m_new)
            p = jnp.concatenate([jnp.exp(x - m_new) for x in parts], axis=1).astype(BF16)
            acc = _tile_lanes(alpha, 2) * acc + jnp.dot(p, v, preferred_element_type=F32)
            out.append((m_new, acc))
        return tuple(out)

    init = tuple((jnp.full((grp, LANES), MASKED, F32), jnp.zeros((grp, 2 * LANES), F32)) for _ in range(ngrp))
    res = lax.fori_loop(0, nch, att_step, init)
    for g in range(ngrp):
        _, acc = res[g]
        o = acc[:, :LANES] / acc[:, LANES:]
        for t in range(DSA_GROUP):
            hl = slice((g * DSA_GROUP + t) * LANES, (g * DSA_GROUP + t + 1) * LANES)
            o_ref[:, hl] = (o[t * bq:(t + 1) * bq] * _silu(gate_ref[:, hl].astype(F32))).astype(o_ref.dtype)


def _dsa_attention(proj, small, dsa_tabs_q, dsa_tabs_k, idx_tabs, batch, seq):
    t = batch * seq
    bq = DSA_BQ
    nq = seq // bq
    topk = min(TOPK_MAX, seq // 4)
    rows = DSA_HEADS * bq
    idx_w = IDX_HEADS * IDX_DIM
    c_dsa_q = 2 * DSA_WIDTH + 3 * FOX_WIDTH
    qrow = lambda b, i: (i, 0)
    full = lambda b, i: (0, 0)
    return pl.pallas_call(
        functools.partial(_dsa_kernel, seq=seq, topk=topk),
        grid=(batch, nq),
        in_specs=[pl.BlockSpec((bq, DSA_WIDTH), lambda b, i: (b * nq + i, c_dsa_q // DSA_WIDTH)),
                  pl.BlockSpec((bq, idx_w), lambda b, i: (b * nq + i, (c_dsa_q + DSA_WIDTH) // idx_w)),
                  pl.BlockSpec((bq, 3 * LANES), lambda b, i: (b * nq + i, 0)),
                  pl.BlockSpec((seq, 3 * LANES), lambda b, i: (b, 0)),
                  pl.BlockSpec((bq, DSA_WIDTH), lambda b, i: (b * nq + i, 0)),
                  pl.BlockSpec((bq, LANES), qrow), pl.BlockSpec((bq, LANES), qrow),
                  pl.BlockSpec((seq, LANES), full), pl.BlockSpec((seq, LANES), full),
                  pl.BlockSpec((bq, LANES), qrow), pl.BlockSpec((bq, LANES), qrow),
                  pl.BlockSpec((seq, LANES), full), pl.BlockSpec((seq, LANES), full)],
        out_specs=pl.BlockSpec((bq, DSA_WIDTH), lambda b, i: (b * nq + i, 0)),
        out_shape=jax.ShapeDtypeStruct((t, DSA_WIDTH), BF16),
        scratch_shapes=[pltpu.VMEM((seq, LANES), BF16), pltpu.VMEM((seq, 2 * LANES), BF16),
                        pltpu.VMEM((seq, LANES), BF16), pltpu.VMEM((seq // DSA_CK, bq, DSA_CK), F32),
                        pltpu.VMEM((rows, LANES), BF16), pltpu.VMEM((rows, LANES), BF16),
                        pltpu.VMEM((rows, LANES), F32)],
        compiler_params=_params("arbitrary", "arbitrary"),
        name="dsa_attention",
    )(proj, proj, small, small, proj, *dsa_tabs_q, *dsa_tabs_k, *idx_tabs, *idx_tabs)


def _even_layer(x2, batch, seq, norm_g, w_in, q_norm_g, w_uq, kv_norm_g, w_ukv, diff_lambda, subln_g, w_out,
                lambda_init, bq):
    wt = w_in.T
    n_lat = MLA_Q_RANK + MLA_KV_RANK + MLA_ROPE
    n_big = 2 * MLA_WIDTH + 3 * DIFF_WIDTH
    lat_cols = 2 * 1024
    uq = w_uq.astype(BF16).reshape(MLA_Q_RANK, MLA_HEADS, MLA_NOPE + MLA_ROPE)
    uq = jnp.concatenate([uq, jnp.zeros((MLA_Q_RANK, MLA_HEADS, 2 * LANES - MLA_NOPE - MLA_ROPE), BF16)], axis=2)
    uq = uq.reshape(MLA_Q_RANK, MLA_HEADS * 2 * LANES)
    ukv = w_ukv.astype(BF16).reshape(MLA_KV_RANK, MLA_HEADS, MLA_NOPE + MLA_V)
    ukv = jnp.concatenate([ukv[:, :, :MLA_NOPE].reshape(MLA_KV_RANK, -1), ukv[:, :, MLA_NOPE:].reshape(MLA_KV_RANK, -1)],
                          axis=1)

    h = _rmsnorm(x2, norm_g, BF16)
    proj = _matmul_nt(h, wt, [(n_lat, n_big)], BF16, tm=512, tn=1024, name="even_inproj")
    lat = _matmul_nt(h, wt, [(0, lat_cols)], BF16, tm=512, tn=1024, name="even_latent")
    q = _norm_matmul(lat, 0, q_norm_g, uq, tm=1024, tn=1024, name="mla_q_up")
    kv = _norm_matmul(lat, MLA_Q_RANK // MLA_KV_RANK, kv_norm_g, ukv, tm=1024, tn=1024, name="mla_kv_up")

    mla_scale = (MLA_NOPE + MLA_ROPE) ** -0.5
    o_a = _mla_attention(q, kv, lat, proj, _rope_tables(seq, MLA_ROPE, LANES, mla_scale),
                         _rope_tables(seq, MLA_ROPE, LANES), batch, seq, bq=bq)
    rot = DIFF_DIM // ROPE_FRACTION
    o_b = _diff_attention(proj, diff_lambda, subln_g, _rope_tables(seq, rot, LANES, DIFF_DIM ** -0.5),
                          _rope_tables(seq, rot, LANES), batch, seq, lambda_init, bq=bq)
    return _outproj(o_a, o_b, w_out.astype(BF16), x2, tm=1024, tn=1024)


def _odd_layer(x2, batch, seq, norm_g, w_in, forget_bias, w_out, bq):
    wt = w_in.T
    c_dsa_k = DSA_WIDTH
    c_idx_q = c_dsa_k + 2 * DSA_DIM
    c_idx_k = c_idx_q + IDX_HEADS * IDX_DIM
    c_gate_c = c_idx_k + IDX_DIM + IDX_HEADS
    c_fox_f = c_gate_c + DSA_WIDTH + 3 * FOX_WIDTH
    c_gate_d = c_fox_f + FOX_HEADS
    big_rows = [(c_gate_c, DSA_WIDTH + 3 * FOX_WIDTH), (c_gate_d, FOX_WIDTH), (0, DSA_WIDTH),
                (c_idx_q, IDX_HEADS * IDX_DIM)]
    n_small = 2 * DSA_DIM + IDX_DIM + IDX_HEADS + FOX_HEADS
    w_small = jnp.concatenate([wt[c_dsa_k:c_dsa_k + 2 * DSA_DIM], wt[c_idx_k:c_idx_k + IDX_DIM + IDX_HEADS],
                               wt[c_fox_f:c_fox_f + FOX_HEADS],
                               jnp.zeros((3 * LANES - n_small, wt.shape[1]), wt.dtype)], axis=0)

    h = _rmsnorm(x2, norm_g, BF16)
    proj = _matmul_nt(h, wt, big_rows, BF16, tm=512, tn=1024, name="odd_inproj")
    small = _matmul_nt(h, w_small, [(0, 3 * LANES)], F32, tm=1024, tn=3 * LANES, name="odd_small")

    rot = DSA_DIM // ROPE_FRACTION
    o_c = _dsa_attention(proj, small, _rope_tables(seq, rot, LANES, DSA_DIM ** -0.5), _rope_tables(seq, rot, LANES),
                         _rope_tables(seq, IDX_DIM // ROPE_FRACTION, IDX_DIM), batch, seq)
    cum = _logf_cumsum(small, forget_bias, batch, seq)
    cum_t = cum[:, FOX_F_LANE:FOX_F_LANE + FOX_HEADS].reshape(batch, seq, FOX_HEADS).transpose(0, 2, 1)
    cum_t = cum_t.reshape(batch * FOX_HEADS, seq // FLASH_SUB, FLASH_SUB)
    o_d = _fox_attention(proj, cum, cum_t, batch, seq, bq=bq)
    return _outproj(o_c, o_d, w_out.astype(BF16), x2, tm=1024, tn=1024)


def kernel(x, even_norm, even_w_in, mla_q_norm, mla_w_uq, mla_kv_norm, mla_w_ukv, diff_lambda, diff_subln, even_w_out, odd_norm, odd_w_in, fox_forget_bias, odd_w_out, final_norm):
    batch, seq, d = x.shape
    bq = min(1024, seq)
    h = x.reshape(batch * seq, d)
    depth = even_norm.shape[0] + odd_norm.shape[0]
    for layer in range(depth):
        i = layer // 2
        if layer % 2 == 0:
            lambda_init = 0.8 - 0.6 * math.exp(-0.3 * layer)
            h = _even_layer(h, batch, seq, even_norm[i], even_w_in[i], mla_q_norm[i], mla_w_uq[i], mla_kv_norm[i],
                            mla_w_ukv[i], diff_lambda[i], diff_subln[i], even_w_out[i], lambda_init, bq)
        else:
            h = _odd_layer(h, batch, seq, odd_norm[i], odd_w_in[i], fox_forget_bias[i], odd_w_out[i], bq)
    return _rmsnorm(h, final_norm, x.dtype).reshape(batch, seq, d)
```

```python
import functools
import math

import jax
import jax.numpy as jnp
from jax import lax
from jax.experimental import pallas as pl
from jax.experimental.pallas import tpu as pltpu

F32 = jnp.float32
BF16 = jnp.bfloat16

ROPE_THETA = 500000.0
NORM_EPS = 1e-6
ROPE_FRACTION = 4

MLA_HEADS, MLA_NOPE, MLA_ROPE, MLA_V = 16, 128, 64, 128
MLA_Q_RANK, MLA_KV_RANK = 1024, 512
DIFF_HEADS, DIFF_DIM = 8, 128
DSA_HEADS, DSA_DIM = 16, 128
IDX_HEADS, IDX_DIM = 16, 64
TOPK_MAX = 256
FOX_HEADS, FOX_DIM = 16, 128

MLA_WIDTH = MLA_HEADS * MLA_V
DIFF_WIDTH = DIFF_HEADS * 2 * DIFF_DIM
DSA_WIDTH = DSA_HEADS * DSA_DIM
FOX_WIDTH = FOX_HEADS * FOX_DIM

LANES = 128
VMEM_LIMIT_BYTES = 56 * 1024 * 1024
MASKED = -1e30

IDX_W_LANE = IDX_DIM
FOX_F_LANE = IDX_DIM + IDX_HEADS

_NT = (((1,), (1,)), ((), ()))


def _params(*sem):
    return pltpu.CompilerParams(dimension_semantics=sem, vmem_limit_bytes=VMEM_LIMIT_BYTES)


def _rmsnorm_kernel(x_ref, g_ref, o_ref):
    x = x_ref[...].astype(F32)
    y = x * lax.rsqrt(jnp.mean(x * x, axis=-1, keepdims=True) + NORM_EPS)
    o_ref[...] = (y * g_ref[...]).astype(o_ref.dtype)


def _rmsnorm(x, g, out_dtype, tm=256):
    t, d = x.shape
    return pl.pallas_call(
        _rmsnorm_kernel,
        grid=(t // tm,),
        in_specs=[pl.BlockSpec((tm, d), lambda i: (i, 0)), pl.BlockSpec((1, d), lambda i: (0, 0))],
        out_specs=pl.BlockSpec((tm, d), lambda i: (i, 0)),
        out_shape=jax.ShapeDtypeStruct((t, d), out_dtype),
        compiler_params=_params("arbitrary"),
        name="rmsnorm",
    )(x, g.reshape(1, d).astype(F32))


def _mm_nt_kernel(a_ref, bt_ref, o_ref):
    bt = bt_ref[...].astype(BF16)
    o_ref[...] = lax.dot_general(a_ref[...], bt, _NT, preferred_element_type=F32).astype(o_ref.dtype)


SUBLANES = 8


def _matmul_nt(a, bt, row_ranges, out_dtype, *, tm, tn, name):
    m, kdim = a.shape
    starts = []
    for first, n_rows in row_ranges:
        assert first % SUBLANES == 0 and n_rows % tn == 0, (first, n_rows)
        starts += [first + t * tn for t in range(n_rows // tn)]
    n = len(starts) * tn

    def bt_map(j, i):
        q = starts[-1] // SUBLANES
        for t in range(len(starts) - 2, -1, -1):
            q = jnp.where(j <= t, starts[t] // SUBLANES, q)
        return q * SUBLANES, 0

    return pl.pallas_call(
        _mm_nt_kernel,
        grid=(len(starts), m // tm),
        in_specs=[pl.BlockSpec((tm, kdim), lambda j, i: (i, 0)),
                  pl.BlockSpec((pl.Element(tn), pl.Element(kdim)), bt_map)],
        out_specs=pl.BlockSpec((tm, tn), lambda j, i: (i, j)),
        out_shape=jax.ShapeDtypeStruct((m, n), out_dtype),
        compiler_params=_params("arbitrary", "arbitrary"),
        name=name,
    )(a, bt)


def _norm_mm_kernel(a_ref, g_ref, b_ref, o_ref):
    x = a_ref[...].astype(F32)
    y = x * lax.rsqrt(jnp.mean(x * x, axis=-1, keepdims=True) + NORM_EPS) * g_ref[...]
    o_ref[...] = jnp.dot(y.astype(BF16), b_ref[...], preferred_element_type=F32).astype(o_ref.dtype)


def _norm_matmul(a, a_col_block, g, b, *, tm, tn, name):
    m = a.shape[0]
    kdim, n = b.shape
    return pl.pallas_call(
        _norm_mm_kernel,
        grid=(m // tm, n // tn),
        in_specs=[pl.BlockSpec((tm, kdim), lambda i, j: (i, a_col_block)),
                  pl.BlockSpec((1, kdim), lambda i, j: (0, 0)),
                  pl.BlockSpec((kdim, tn), lambda i, j: (0, j))],
        out_specs=pl.BlockSpec((tm, tn), lambda i, j: (i, j)),
        out_shape=jax.ShapeDtypeStruct((m, n), BF16),
        compiler_params=_params("arbitrary", "arbitrary"),
        name=name,
    )(a, g.reshape(1, kdim).astype(F32), b)


def _outproj_kernel(oa_ref, ob_ref, w_ref, res_ref, h_ref):
    ka = oa_ref.shape[1]
    acc = jnp.dot(oa_ref[...], w_ref[:ka, :], preferred_element_type=F32)
    acc = acc + jnp.dot(ob_ref[...], w_ref[ka:, :], preferred_element_type=F32)
    h_ref[...] = acc + res_ref[...]


def _outproj(oa, ob, w, res, *, tm, tn):
    m, ka = oa.shape
    kb = ob.shape[1]
    n = w.shape[1]
    return pl.pallas_call(
        _outproj_kernel,
        grid=(m // tm, n // tn),
        in_specs=[pl.BlockSpec((tm, ka), lambda i, j: (i, 0)),
                  pl.BlockSpec((tm, kb), lambda i, j: (i, 0)),
                  pl.BlockSpec((ka + kb, tn), lambda i, j: (0, j)),
                  pl.BlockSpec((tm, tn), lambda i, j: (i, j))],
        out_specs=pl.BlockSpec((tm, tn), lambda i, j: (i, j)),
        out_shape=jax.ShapeDtypeStruct((m, n), F32),
        compiler_params=_params("arbitrary", "arbitrary"),
        name="outproj",
    )(oa, ob, w, res)


def _rope_tables(seq, rot_dim, period, scale=1.0):
    inv = ROPE_THETA ** (-jnp.arange(0, rot_dim, 2, dtype=F32) / rot_dim)
    ang = jnp.arange(seq, dtype=F32)[:, None] * inv[None, :]
    cos, sin = jnp.cos(ang), jnp.sin(ang)
    ones = jnp.ones((seq, period - rot_dim), F32)
    c = jnp.concatenate([cos, cos, ones], axis=1) * scale
    s = jnp.concatenate([-sin, sin, 0.0 * ones], axis=1) * scale
    reps = LANES // period
    return jnp.tile(c, (1, reps)), jnp.tile(s, (1, reps))


def _rope(x, c, s, half, period):
    lane = lax.broadcasted_iota(jnp.int32, x.shape, 1)
    first = (lane & (period - 1)) < half
    partner = jnp.where(first, pltpu.roll(x, LANES - half, axis=1), pltpu.roll(x, half, axis=1))
    return x * c + partner * s


def _softmax_pv(s, v, carry, mask=None):
    m, l, acc = carry
    if mask is not None:
        s = jnp.where(mask, s, MASKED)
    m_new = jnp.maximum(m, jnp.max(s, axis=1, keepdims=True))
    alpha = jnp.exp(m - m_new)
    p = jnp.exp(s - m_new)
    l = alpha * l + jnp.sum(p, axis=1, keepdims=True)
    acc = alpha * acc + jnp.dot(p.astype(BF16), v, preferred_element_type=F32)
    return m_new, l, acc


def _init_carry(rows, dv):
    return (jnp.full((rows, 1), MASKED, F32), jnp.zeros((rows, 1), F32), jnp.zeros((rows, dv), F32))


FLASH_SUB = 512


def _tri_mask(n):
    row = lax.broadcasted_iota(jnp.int32, (n, n), 0)
    col = lax.broadcasted_iota(jnp.int32, (n, n), 1)
    return row >= col


def _causal_sweep(logits, values, i, bq, n_chain, chains_per_sub, dv):
    sub = FLASH_SUB
    nsub = bq // sub

    def step(j, st):
        v = values(j * nsub, nsub)
        return tuple(_softmax_pv(logits(t, j * nsub, nsub), v, st[t]) for t in range(n_chain))

    st = list(lax.fori_loop(0, i, step, tuple(_init_carry(sub, dv) for _ in range(n_chain))))
    mask = _tri_mask(sub)
    for c in range(nsub):
        blk = i * nsub + c
        v = values(blk, 1)
        for t in range(n_chain):
            r = t // chains_per_sub
            if r >= c:
                st[t] = _softmax_pv(logits(t, blk, 1), v, st[t], mask if r == c else None)
    return st


def _key_rows(blk, nb):
    return pl.ds(pl.multiple_of(blk * FLASH_SUB, FLASH_SUB), nb * FLASH_SUB)


def _silu(g):
    return g * jax.nn.sigmoid(g)


def _mla_kernel(q_ref, kn_ref, kr_ref, v_ref, gate_ref, cq_ref, sq_ref, ck_ref, sk_ref, o_ref, k_scr, *, bq, scale):
    i = pl.program_id(2)
    half = MLA_ROPE // 2
    sub = FLASH_SUB

    @pl.when(i == 0)
    def _():
        k_scr[:, :LANES] = kn_ref[...]
        k_scr[:, LANES:] = _rope(kr_ref[...].astype(F32), ck_ref[...], sk_ref[...], half, LANES).astype(BF16)

    qs = []
    for r in range(bq // sub):
        rs = slice(r * sub, (r + 1) * sub)
        qn = (q_ref[rs, :LANES].astype(F32) * scale).astype(BF16)
        qr = _rope(q_ref[rs, LANES:].astype(F32), cq_ref[rs, :], sq_ref[rs, :], half, LANES).astype(BF16)
        qs.append(jnp.concatenate([qn, qr], axis=1))

    def logits(t, blk, nb):
        return lax.dot_general(qs[t], k_scr[_key_rows(blk, nb), :], _NT, preferred_element_type=F32)

    def values(blk, nb):
        return v_ref[_key_rows(blk, nb), :]

    st = _causal_sweep(logits, values, i, bq, bq // sub, 1, MLA_V)
    for r, (_, l, acc) in enumerate(st):
        rs = slice(r * sub, (r + 1) * sub)
        o_ref[rs, :] = (acc / l * _silu(gate_ref[rs, :].astype(F32))).astype(o_ref.dtype)


def _mla_attention(q, kv, lat, proj, tabs_q, tabs_k, batch, seq, *, bq):
    t = batch * seq
    nq = seq // bq
    kr_block = (MLA_Q_RANK + MLA_KV_RANK) // LANES
    qrow = lambda b, h, i: (i, 0)
    full = lambda b, h, i: (0, 0)
    return pl.pallas_call(
        functools.partial(_mla_kernel, bq=bq, scale=(MLA_NOPE + MLA_ROPE) ** -0.5),
        grid=(batch, MLA_HEADS, nq),
        in_specs=[pl.BlockSpec((bq, 2 * LANES), lambda b, h, i: (b * nq + i, h)),
                  pl.BlockSpec((seq, LANES), lambda b, h, i: (b, h)),
                  pl.BlockSpec((seq, LANES), lambda b, h, i: (b, kr_block)),
                  pl.BlockSpec((seq, LANES), lambda b, h, i: (b, MLA_HEADS + h)),
                  pl.BlockSpec((bq, MLA_V), lambda b, h, i: (b * nq + i, h)),
                  pl.BlockSpec((bq, LANES), qrow), pl.BlockSpec((bq, LANES), qrow),
                  pl.BlockSpec((seq, LANES), full), pl.BlockSpec((seq, LANES), full)],
        out_specs=pl.BlockSpec((bq, MLA_V), lambda b, h, i: (b * nq + i, h)),
        out_shape=jax.ShapeDtypeStruct((t, MLA_WIDTH), BF16),
        scratch_shapes=[pltpu.VMEM((seq, 2 * LANES), BF16)],
        compiler_params=_params("arbitrary", "arbitrary", "arbitrary"),
        name="mla_attention",
    )(q, kv, lat, kv, proj, *tabs_q, *tabs_k)


def _diff_kernel(q_ref, k_ref, v_ref, gate_ref, lam_ref, g_ref, cq_ref, sq_ref, ck_ref, sk_ref, o_ref, k_scr,
                 *, bq, lambda_init):
    i = pl.program_id(2)
    half = DIFF_DIM // ROPE_FRACTION // 2
    sub = FLASH_SUB

    @pl.when(i == 0)
    def _():
        for c in range(2):
            sl = slice(c * LANES, (c + 1) * LANES)
            k_scr[:, sl] = _rope(k_ref[:, sl].astype(F32), ck_ref[...], sk_ref[...], half, LANES).astype(BF16)

    qs = []
    for r in range(bq // sub):
        rs = slice(r * sub, (r + 1) * sub)
        for c in range(2):
            x = q_ref[rs, c * LANES:(c + 1) * LANES].astype(F32)
            qs.append(_rope(x, cq_ref[rs, :], sq_ref[rs, :], half, LANES).astype(BF16))

    def logits(t, blk, nb):
        c = t % 2
        return lax.dot_general(qs[t], k_scr[_key_rows(blk, nb), c * LANES:(c + 1) * LANES], _NT,
                               preferred_element_type=F32)

    def values(blk, nb):
        return v_ref[_key_rows(blk, nb), :]

    st = _causal_sweep(logits, values, i, bq, 2 * (bq // sub), 2, 2 * DIFF_DIM)

    lp = lam_ref[...]
    lam = (jnp.exp(jnp.sum(lp[0:1] * lp[1:2], axis=1, keepdims=True))
           - jnp.exp(jnp.sum(lp[2:3] * lp[3:4], axis=1, keepdims=True)) + lambda_init)
    for r in range(bq // sub):
        rs = slice(r * sub, (r + 1) * sub)
        (_, l0, acc0), (_, l1, acc1) = st[2 * r], st[2 * r + 1]
        o = acc0 / l0 - lam * (acc1 / l1)
        o = o * lax.rsqrt(jnp.mean(o * o, axis=-1, keepdims=True) + NORM_EPS) * g_ref[...]
        o_ref[rs, :] = (o * (1.0 - lambda_init) * _silu(gate_ref[rs, :].astype(F32))).astype(o_ref.dtype)


def _diff_attention(proj, lam_params, subln_g, tabs_q, tabs_k, batch, seq, lambda_init, *, bq):
    t = batch * seq
    nq = seq // bq
    w = 2 * DIFF_DIM
    q0 = MLA_WIDTH // w
    k0, v0, g0 = q0 + DIFF_HEADS, q0 + 2 * DIFF_HEADS, q0 + 3 * DIFF_HEADS
    qrow = lambda b, h, i: (i, 0)
    full = lambda b, h, i: (0, 0)
    return pl.pallas_call(
        functools.partial(_diff_kernel, bq=bq, lambda_init=lambda_init),
        grid=(batch, DIFF_HEADS, nq),
        in_specs=[pl.BlockSpec((bq, w), lambda b, h, i: (b * nq + i, q0 + h)),
                  pl.BlockSpec((seq, w), lambda b, h, i: (b, k0 + h)),
                  pl.BlockSpec((seq, w), lambda b, h, i: (b, v0 + h)),
                  pl.BlockSpec((bq, w), lambda b, h, i: (b * nq + i, g0 + h)),
                  pl.BlockSpec((4, DIFF_DIM), full),
                  pl.BlockSpec((1, w), full),
                  pl.BlockSpec((bq, LANES), qrow), pl.BlockSpec((bq, LANES), qrow),
                  pl.BlockSpec((seq, LANES), full), pl.BlockSpec((seq, LANES), full)],
        out_specs=pl.BlockSpec((bq, w), lambda b, h, i: (b * nq + i, h)),
        out_shape=jax.ShapeDtypeStruct((t, DIFF_WIDTH), BF16),
        scratch_shapes=[pltpu.VMEM((seq, w), BF16)],
        compiler_params=_params("arbitrary", "arbitrary", "arbitrary"),
        name="diff_attention",
    )(proj, proj, proj, proj, lam_params.astype(F32), subln_g.reshape(1, w).astype(F32), *tabs_q, *tabs_k)


def _fox_kernel(q_ref, k_ref, v_ref, gate_ref, cum_ref, cumt_ref, o_ref, *, bq, scale):
    h = pl.program_id(1)
    i = pl.program_id(2)
    sub = FLASH_SUB
    lane = lax.broadcasted_iota(jnp.int32, (sub, LANES), 1)
    qs, cqs = [], []
    for r in range(bq // sub):
        rs = slice(r * sub, (r + 1) * sub)
        qs.append((q_ref[rs, :].astype(F32) * scale).astype(BF16))
        cqs.append(jnp.sum(jnp.where(lane == FOX_F_LANE + h, cum_ref[rs, :], 0.0), axis=1, keepdims=True))

    def logits(t, blk, nb):
        s = lax.dot_general(qs[t], k_ref[_key_rows(blk, nb), :], _NT, preferred_element_type=F32)
        ck = jnp.concatenate([cumt_ref[0, pl.ds(blk + n, 1), :] for n in range(nb)], axis=1)
        return s + (cqs[t] - ck)

    def values(blk, nb):
        return v_ref[_key_rows(blk, nb), :]

    st = _causal_sweep(logits, values, i, bq, bq // sub, 1, FOX_DIM)
    for r, (_, l, acc) in enumerate(st):
        rs = slice(r * sub, (r + 1) * sub)
        o_ref[rs, :] = (acc / l * _silu(gate_ref[rs, :].astype(F32))).astype(o_ref.dtype)


def _fox_attention(proj, cum, cum_t, batch, seq, *, bq):
    t = batch * seq
    nq = seq // bq
    q0 = DSA_WIDTH // LANES
    k0, v0, g0 = q0 + FOX_HEADS, q0 + 2 * FOX_HEADS, q0 + 3 * FOX_HEADS
    return pl.pallas_call(
        functools.partial(_fox_kernel, bq=bq, scale=FOX_DIM ** -0.5),
        grid=(batch, FOX_HEADS, nq),
        in_specs=[pl.BlockSpec((bq, LANES), lambda b, h, i: (b * nq + i, q0 + h)),
                  pl.BlockSpec((seq, LANES), lambda b, h, i: (b, k0 + h)),
                  pl.BlockSpec((seq, LANES), lambda b, h, i: (b, v0 + h)),
                  pl.BlockSpec((bq, LANES), lambda b, h, i: (b * nq + i, g0 + h)),
                  pl.BlockSpec((bq, LANES), lambda b, h, i: (b * nq + i, 0)),
                  pl.BlockSpec((1, seq // FLASH_SUB, FLASH_SUB), lambda b, h, i: (b * FOX_HEADS + h, 0, 0))],
        out_specs=pl.BlockSpec((bq, FOX_DIM), lambda b, h, i: (b * nq + i, h)),
        out_shape=jax.ShapeDtypeStruct((t, FOX_WIDTH), BF16),
        compiler_params=_params("arbitrary", "arbitrary", "arbitrary"),
        name="fox_attention",
    )(proj, proj, proj, proj, cum, cum_t)


def _logf_cumsum_kernel(f_ref, b_ref, o_ref, carry_ref, *, tb):
    @pl.when(pl.program_id(1) == 0)
    def _():
        carry_ref[...] = jnp.zeros_like(carry_ref)

    x = f_ref[...] + b_ref[...]
    logf = jnp.minimum(x, 0.0) - jnp.log1p(jnp.exp(-jnp.abs(x)))
    row = lax.broadcasted_iota(jnp.int32, (tb, tb), 0)
    col = lax.broadcasted_iota(jnp.int32, (tb, tb), 1)
    tri = jnp.where(row >= col, 1.0, 0.0).astype(BF16)
    hi = logf.astype(BF16)
    r1 = logf - hi.astype(F32)
    mid = r1.astype(BF16)
    lo = (r1 - mid.astype(F32)).astype(BF16)
    cum = (jnp.dot(tri, hi, preferred_element_type=F32) + jnp.dot(tri, mid, preferred_element_type=F32)
           + jnp.dot(tri, lo, preferred_element_type=F32)) + carry_ref[...]
    o_ref[...] = cum
    carry_ref[...] = cum[tb - 1:tb, :]


def _logf_cumsum(small, forget_bias, batch, seq, *, tb=512):
    t = batch * seq
    nb = seq // tb
    bias = jnp.zeros((1, LANES), F32).at[0, FOX_F_LANE:FOX_F_LANE + FOX_HEADS].set(forget_bias.astype(F32))
    return pl.pallas_call(
        functools.partial(_logf_cumsum_kernel, tb=tb),
        grid=(batch, nb),
        in_specs=[pl.BlockSpec((tb, LANES), lambda b, i: (b * nb + i, 2)),
                  pl.BlockSpec((1, LANES), lambda b, i: (0, 0))],
        out_specs=pl.BlockSpec((tb, LANES), lambda b, i: (b * nb + i, 0)),
        out_shape=jax.ShapeDtypeStruct((t, LANES), F32),
        scratch_shapes=[pltpu.VMEM((1, LANES), F32)],
        compiler_params=_params("arbitrary", "arbitrary"),
        name="logf_cumsum",
    )(small, bias)


DSA_CK = 512
DSA_BQ = 256
DSA_GROUP = 2
DSA_MAX_ITERS = 640
DSA_PROBES_PER_CHECK = 4


def _row_total(x):
    return jnp.broadcast_to(jnp.sum(x, axis=1, keepdims=True), x.shape)


def _tile_lanes(x, n):
    return jnp.concatenate([x] * n, axis=1)


def _dsa_kernel(qc_ref, qi_ref, wq_ref, kv_ref, gate_ref, cdq_ref, sdq_ref, cdk_ref, sdk_ref, ciq_ref, siq_ref,
                cik_ref, sik_ref, o_ref, kc_scr, vc_scr, ki_scr, idx_scr, q_scr, qi_scr, w_scr, *, seq, topk):
    i = pl.program_id(1)
    bq, ck = DSA_BQ, DSA_CK
    dhalf = DSA_DIM // ROPE_FRACTION // 2
    ihalf = IDX_DIM // ROPE_FRACTION // 2
    rows = DSA_HEADS * bq
    lane = lax.broadcasted_iota(jnp.int32, (bq, LANES), 1)

    @pl.when(i == 0)
    def _():
        tb = 512

        def prep(r, _):
            off = pl.multiple_of(r * tb, tb)
            sl = pl.ds(off, tb)
            kc_scr[sl, :] = _rope(kv_ref[sl, 0:LANES], cdk_ref[sl, :], sdk_ref[sl, :], dhalf, LANES).astype(BF16)
            vc_scr[sl, :LANES] = kv_ref[sl, LANES:2 * LANES].astype(BF16)
            vc_scr[sl, LANES:] = jnp.ones((tb, LANES), BF16)
            lane_t = lax.broadcasted_iota(jnp.int32, (tb, LANES), 1)
            ki = jnp.where(lane_t < IDX_DIM, kv_ref[sl, 2 * LANES:3 * LANES], 0.0)
            ki = _rope(ki, cik_ref[sl, :], sik_ref[sl, :], ihalf, IDX_DIM)
            ki_scr[sl, :] = (ki + pltpu.roll(ki, IDX_DIM, axis=1)).astype(BF16)
            return 0

        lax.fori_loop(0, seq // tb, prep, 0)

    for h in range(DSA_HEADS):
        xh = qc_ref[:, h * LANES:(h + 1) * LANES].astype(F32)
        q_scr[h * bq:(h + 1) * bq, :] = _rope(xh, cdq_ref[...], sdq_ref[...], dhalf, LANES).astype(BF16)
    for p in range(IDX_HEADS // 2):
        xp = _rope(qi_ref[:, p * LANES:(p + 1) * LANES].astype(F32), ciq_ref[...], siq_ref[...], ihalf, IDX_DIM)
        qi_scr[(2 * p) * bq:(2 * p + 1) * bq, :] = jnp.where(lane < IDX_DIM, xp, 0.0).astype(BF16)
        qi_scr[(2 * p + 1) * bq:(2 * p + 2) * bq, :] = jnp.where(lane >= IDX_DIM, xp, 0.0).astype(BF16)
    w = wq_ref[:, 2 * LANES:3 * LANES] * ((IDX_HEADS ** -0.5) * (IDX_DIM ** -0.5))
    for h in range(IDX_HEADS):
        w_scr[h * bq:(h + 1) * bq, :] = _row_total(jnp.where(lane == IDX_W_LANE + h, w, 0.0))

    nch = i // (ck // bq) + 1
    qpos = i * bq + lax.broadcasted_iota(jnp.int32, (bq, LANES), 0)
    inf = jnp.full((bq, LANES), jnp.inf, F32)

    def idx_step(j, carry):
        mn, mx = carry
        off = pl.multiple_of(j * ck, ck)
        r = lax.dot_general(qi_scr[...], ki_scr[pl.ds(off, ck), :], _NT, preferred_element_type=F32)
        halves = []
        for c in range(ck // LANES):
            sc = None
            for h in range(IDX_HEADS):
                term = jnp.maximum(r[h * bq:(h + 1) * bq, c * LANES:(c + 1) * LANES], 0.0) * w_scr[h * bq:(h + 1) * bq, :]
                sc = term if sc is None else sc + term
            valid = (off + c * LANES + lane) <= qpos
            halves.append(jnp.where(valid, sc, -inf))
            mn = jnp.minimum(mn, jnp.where(valid, sc, inf))
            mx = jnp.maximum(mx, jnp.where(valid, sc, -inf))
        idx_scr[j] = jnp.concatenate(halves, axis=1)
        return mn, mx

    mn, mx = lax.fori_loop(0, nch, idx_step, (inf, -inf))
    mn = jnp.broadcast_to(jnp.min(mn, axis=1, keepdims=True), mn.shape)
    mx = jnp.broadcast_to(jnp.max(mx, axis=1, keepdims=True), mx.shape)

    def count_ge(t):
        nslab = bq // LANES

        def body(j, cs):
            out = []
            for u in range(nslab):
                rs = slice(u * LANES, (u + 1) * LANES)
                x = idx_scr[j, rs, :]
                c = cs[u]
                for half in range(ck // LANES):
                    c = c + jnp.where(x[:, half * LANES:(half + 1) * LANES] >= t[rs], 1.0, 0.0)
                out.append(c)
            return tuple(out)

        cs = lax.fori_loop(0, nch, body, tuple(jnp.zeros((LANES, LANES), F32) for _ in range(nslab)))
        return _row_total(jnp.concatenate(cs, axis=0))

    kf = float(topk)
    nvalid = (qpos + 1).astype(F32)
    c_mx = count_ge(mx)
    few = nvalid <= kf
    top_tie = (~few) & (c_mx >= kf)
    lo = jnp.where(top_tie, mx, mn)
    c_lo = jnp.where(top_tie, c_mx, nvalid)
    hi = jnp.where(top_tie, inf, mx)
    c_hi = jnp.where(top_tie, 0.0, c_mx)
    done = jnp.where(few | top_tie | (c_lo == kf), 1.0, 0.0)

    def search_cond(st):
        it, _, _, _, _, done = st
        return jnp.logical_and(it < DSA_MAX_ITERS, jnp.min(done) < 0.5)

    def probe(st):
        lo, hi, c_lo, c_hi, done = st
        mid = 0.5 * lo + 0.5 * hi
        adjacent = (mid <= lo) | (mid >= hi)
        c = count_ge(mid)
        ge = c >= kf
        upd = (done < 0.5) & (~adjacent)
        lo = jnp.where(upd & ge, mid, lo)
        c_lo = jnp.where(upd & ge, c, c_lo)
        hi = jnp.where(upd & (~ge), mid, hi)
        c_hi = jnp.where(upd & (~ge), c, c_hi)
        done = jnp.where(adjacent | (c_lo == kf), 1.0, done)
        return lo, hi, c_lo, c_hi, done

    def search_body(st):
        it, rest = st[0], st[1:]
        for _ in range(DSA_PROBES_PER_CHECK):
            rest = probe(rest)
        return (it + DSA_PROBES_PER_CHECK,) + tuple(rest)

    _, lo, hi, c_lo, c_hi, _ = lax.while_loop(search_cond, search_body, (jnp.int32(0), lo, hi, c_lo, c_hi, done))

    nl = ck // LANES
    lo_w = _tile_lanes(lo, nl)

    def write_plain():
        def body(j, _):
            idx_scr[j] = jnp.where(idx_scr[j] >= lo_w, 0.0, MASKED)
            return 0
        lax.fori_loop(0, nch, body, 0)

    def write_ties():
        hi_w = _tile_lanes(hi, nl)
        need = _tile_lanes(kf - c_hi, nl)
        urow = lax.broadcasted_iota(jnp.int32, (ck, ck), 0)
        ucol = lax.broadcasted_iota(jnp.int32, (ck, ck), 1)
        upper = jnp.where(urow < ucol, 1.0, 0.0).astype(BF16)

        def body(j, before):
            x = idx_scr[j]
            above = x >= hi_w
            eq = jnp.where((x >= lo_w) & (~above), 1.0, 0.0)
            prefix = jnp.dot(eq.astype(BF16), upper, preferred_element_type=F32) + _tile_lanes(before, nl)
            sel = above | ((eq > 0.5) & (prefix < need))
            idx_scr[j] = jnp.where(sel, 0.0, MASKED)
            tot = eq[:, :LANES]
            for c in range(1, nl):
                tot = tot + eq[:, c * LANES:(c + 1) * LANES]
            return before + _row_total(tot)

        lax.fori_loop(0, nch, body, jnp.zeros((bq, LANES), F32))

    lax.cond(jnp.max(c_lo) > kf, write_ties, write_plain)

    grp = DSA_GROUP * bq
    ngrp = rows // grp

    def att_step(j, carry):
        off = pl.multiple_of(j * ck, ck)
        k = kc_scr[pl.ds(off, ck), :]
        v = vc_scr[pl.ds(off, ck), :]
        bias = jnp.concatenate([idx_scr[j]] * DSA_GROUP, axis=0)
        out = []
        for g in range(ngrp):
            m, acc = carry[g]
            s = lax.dot_general(q_scr[g * grp:(g + 1) * grp, :], k, _NT, preferred_element_type=F32)
            parts = [s[:, c * LANES:(c + 1) * LANES] + bias[:, c * LANES:(c + 1) * LANES] for c in range(nl)]
            mc = parts[0]
            for c in range(1, nl):
                mc = jnp.maximum(mc, parts[c])
            m_new = jnp.maximum(m, jnp.max(mc, axis=1, keepdims=True))
            alpha = jnp.exp(m - m_new)
            p = jnp.concatenate([jnp.exp(x - m_new) for x in parts], axis=1).astype(BF16)
            acc = _tile_lanes(alpha, 2) * acc + jnp.dot(p, v, preferred_element_type=F32)
            out.append((m_new, acc))
        return tuple(out)

    init = tuple((jnp.full((grp, LANES), MASKED, F32), jnp.zeros((grp, 2 * LANES), F32)) for _ in range(ngrp))
    res = lax.fori_loop(0, nch, att_step, init)
    for g in range(ngrp):
        _, acc = res[g]
        o = acc[:, :LANES] / acc[:, LANES:]
        for t in range(DSA_GROUP):
            hl = slice((g * DSA_GROUP + t) * LANES, (g * DSA_GROUP + t + 1) * LANES)
            o_ref[:, hl] = (o[t * bq:(t + 1) * bq] * _silu(gate_ref[:, hl].astype(F32))).astype(o_ref.dtype)


def _dsa_attention(proj, small, dsa_tabs_q, dsa_tabs_k, idx_tabs, batch, seq):
    t = batch * seq
    bq = DSA_BQ
    nq = seq // bq
    topk = min(TOPK_MAX, seq // 4)
    rows = DSA_HEADS * bq
    idx_w = IDX_HEADS * IDX_DIM
    c_dsa_q = 2 * DSA_WIDTH + 3 * FOX_WIDTH
    qrow = lambda b, i: (i, 0)
    full = lambda b, i: (0, 0)
    return pl.pallas_call(
        functools.partial(_dsa_kernel, seq=seq, topk=topk),
        grid=(batch, nq),
        in_specs=[pl.BlockSpec((bq, DSA_WIDTH), lambda b, i: (b * nq + i, c_dsa_q // DSA_WIDTH)),
                  pl.BlockSpec((bq, idx_w), lambda b, i: (b * nq + i, (c_dsa_q + DSA_WIDTH) // idx_w)),
                  pl.BlockSpec((bq, 3 * LANES), lambda b, i: (b * nq + i, 0)),
                  pl.BlockSpec((seq, 3 * LANES), lambda b, i: (b, 0)),
                  pl.BlockSpec((bq, DSA_WIDTH), lambda b, i: (b * nq + i, 0)),
                  pl.BlockSpec((bq, LANES), qrow), pl.BlockSpec((bq, LANES), qrow),
                  pl.BlockSpec((seq, LANES), full), pl.BlockSpec((seq, LANES), full),
                  pl.BlockSpec((bq, LANES), qrow), pl.BlockSpec((bq, LANES), qrow),
                  pl.BlockSpec((seq, LANES), full), pl.BlockSpec((seq, LANES), full)],
        out_specs=pl.BlockSpec((bq, DSA_WIDTH), lambda b, i: (b * nq + i, 0)),
        out_shape=jax.ShapeDtypeStruct((t, DSA_WIDTH), BF16),
        scratch_shapes=[pltpu.VMEM((seq, LANES), BF16), pltpu.VMEM((seq, 2 * LANES), BF16),
                        pltpu.VMEM((seq, LANES), BF16), pltpu.VMEM((seq // DSA_CK, bq, DSA_CK), F32),
                        pltpu.VMEM((rows, LANES), BF16), pltpu.VMEM((rows, LANES), BF16),
                        pltpu.VMEM((rows, LANES), F32)],
        compiler_params=_params("arbitrary", "arbitrary"),
        name="dsa_attention",
    )(proj, proj, small, small, proj, *dsa_tabs_q, *dsa_tabs_k, *idx_tabs, *idx_tabs)


def _even_layer(x2, batch, seq, norm_g, w_in, q_norm_g, w_uq, kv_norm_g, w_ukv, diff_lambda, subln_g, w_out,
                lambda_init, bq):
    wt = w_in.T
    n_lat = MLA_Q_RANK + MLA_KV_RANK + MLA_ROPE
    n_big = 2 * MLA_WIDTH + 3 * DIFF_WIDTH
    lat_cols = 2 * 1024
    uq = w_uq.astype(BF16).reshape(MLA_Q_RANK, MLA_HEADS, MLA_NOPE + MLA_ROPE)
    uq = jnp.concatenate([uq, jnp.zeros((MLA_Q_RANK, MLA_HEADS, 2 * LANES - MLA_NOPE - MLA_ROPE), BF16)], axis=2)
    uq = uq.reshape(MLA_Q_RANK, MLA_HEADS * 2 * LANES)
    ukv = w_ukv.astype(BF16).reshape(MLA_KV_RANK, MLA_HEADS, MLA_NOPE + MLA_V)
    ukv = jnp.concatenate([ukv[:, :, :MLA_NOPE].reshape(MLA_KV_RANK, -1), ukv[:, :, MLA_NOPE:].reshape(MLA_KV_RANK, -1)],
                          axis=1)

    h = _rmsnorm(x2, norm_g, BF16)
    proj = _matmul_nt(h, wt, [(n_lat, n_big)], BF16, tm=512, tn=1024, name="even_inproj")
    lat = _matmul_nt(h, wt, [(0, lat_cols)], BF16, tm=512, tn=1024, name="even_latent")
    q = _norm_matmul(lat, 0, q_norm_g, uq, tm=1024, tn=1024, name="mla_q_up")
    kv = _norm_matmul(lat, MLA_Q_RANK // MLA_KV_RANK, kv_norm_g, ukv, tm=1024, tn=1024, name="mla_kv_up")

    mla_scale = (MLA_NOPE + MLA_ROPE) ** -0.5
    o_a = _mla_attention(q, kv, lat, proj, _rope_tables(seq, MLA_ROPE, LANES, mla_scale),
                         _rope_tables(seq, MLA_ROPE, LANES), batch, seq, bq=bq)
    rot = DIFF_DIM // ROPE_FRACTION
    o_b = _diff_attention(proj, diff_lambda, subln_g, _rope_tables(seq, rot, LANES, DIFF_DIM ** -0.5),
                          _rope_tables(seq, rot, LANES), batch, seq, lambda_init, bq=bq)
    return _outproj(o_a, o_b, w_out.astype(BF16), x2, tm=1024, tn=1024)


def _odd_layer(x2, batch, seq, norm_g, w_in, forget_bias, w_out, bq):
    wt = w_in.T
    c_dsa_k = DSA_WIDTH
    c_idx_q = c_dsa_k + 2 * DSA_DIM
    c_idx_k = c_idx_q + IDX_HEADS * IDX_DIM
    c_gate_c = c_idx_k + IDX_DIM + IDX_HEADS
    c_fox_f = c_gate_c + DSA_WIDTH + 3 * FOX_WIDTH
    c_gate_d = c_fox_f + FOX_HEADS
    big_rows = [(c_gate_c, DSA_WIDTH + 3 * FOX_WIDTH), (c_gate_d, FOX_WIDTH), (0, DSA_WIDTH),
                (c_idx_q, IDX_HEADS * IDX_DIM)]
    n_small = 2 * DSA_DIM + IDX_DIM + IDX_HEADS + FOX_HEADS
    w_small = jnp.concatenate([wt[c_dsa_k:c_dsa_k + 2 * DSA_DIM], wt[c_idx_k:c_idx_k + IDX_DIM + IDX_HEADS],
                               wt[c_fox_f:c_fox_f + FOX_HEADS],
                               jnp.zeros((3 * LANES - n_small, wt.shape[1]), wt.dtype)], axis=0)

    h = _rmsnorm(x2, norm_g, BF16)
    proj = _matmul_nt(h, wt, big_rows, BF16, tm=512, tn=1024, name="odd_inproj")
    small = _matmul_nt(h, w_small, [(0, 3 * LANES)], F32, tm=1024, tn=3 * LANES, name="odd_small")

    rot = DSA_DIM // ROPE_FRACTION
    o_c = _dsa_attention(proj, small, _rope_tables(seq, rot, LANES, DSA_DIM ** -0.5), _rope_tables(seq, rot, LANES),
                         _rope_tables(seq, IDX_DIM // ROPE_FRACTION, IDX_DIM), batch, seq)
    cum = _logf_cumsum(small, forget_bias, batch, seq)
    cum_t = cum[:, FOX_F_LANE:FOX_F_LANE + FOX_HEADS].reshape(batch, seq, FOX_HEADS).transpose(0, 2, 1)
    cum_t = cum_t.reshape(batch * FOX_HEADS, seq // FLASH_SUB, FLASH_SUB)
    o_d = _fox_attention(proj, cum, cum_t, batch, seq, bq=bq)
    return _outproj(o_c, o_d, w_out.astype(BF16), x2, tm=1024, tn=1024)


def kernel(x, even_norm, even_w_in, mla_q_norm, mla_w_uq, mla_kv_norm, mla_w_ukv, diff_lambda, diff_subln, even_w_out, odd_norm, odd_w_in, fox_forget_bias, odd_w_out, final_norm):
    batch, seq, d = x.shape
    bq = min(1024, seq)
    h = x.reshape(batch * seq, d)
    depth = even_norm.shape[0] + odd_norm.shape[0]
    for layer in range(depth):
        i = layer // 2
        if layer % 2 == 0:
            lambda_init = 0.8 - 0.6 * math.exp(-0.3 * layer)
            h = _even_layer(h, batch, seq, even_norm[i], even_w_in[i], mla_q_norm[i], mla_w_uq[i], mla_kv_norm[i],
                            mla_w_ukv[i], diff_lambda[i], diff_subln[i], even_w_out[i], lambda_init, bq)
        else:
            h = _odd_layer(h, batch, seq, odd_norm[i], odd_w_in[i], fox_forget_bias[i], odd_w_out[i], bq)
    return _rmsnorm(h, final_norm, x.dtype).reshape(batch, seq, d)
```

```python
import functools
import math

import jax
import jax.numpy as jnp
from jax import lax
from jax.experimental import pallas as pl
from jax.experimental.pallas import tpu as pltpu

F32 = jnp.float32
BF16 = jnp.bfloat16

ROPE_THETA = 500000.0
NORM_EPS = 1e-6
ROPE_FRACTION = 4

MLA_HEADS, MLA_NOPE, MLA_ROPE, MLA_V = 16, 128, 64, 128
MLA_Q_RANK, MLA_KV_RANK = 1024, 512
DIFF_HEADS, DIFF_DIM = 8, 128
DSA_HEADS, DSA_DIM = 16, 128
IDX_HEADS, IDX_DIM = 16, 64
TOPK_MAX = 256
FOX_HEADS, FOX_DIM = 16, 128

MLA_WIDTH = MLA_HEADS * MLA_V
DIFF_WIDTH = DIFF_HEADS * 2 * DIFF_DIM
DSA_WIDTH = DSA_HEADS * DSA_DIM
FOX_WIDTH = FOX_HEADS * FOX_DIM

LANES = 128
VMEM_LIMIT_BYTES = 56 * 1024 * 1024
MATMUL_VMEM_LIMIT_BYTES = 58 * 1024 * 1024
MASKED = -1e30

IDX_W_LANE = IDX_DIM
FOX_F_LANE = IDX_DIM + IDX_HEADS

_NT = (((1,), (1,)), ((), ()))


def _params(*sem):
    return pltpu.CompilerParams(dimension_semantics=sem, vmem_limit_bytes=VMEM_LIMIT_BYTES)


def _rmsnorm_kernel(x_ref, g_ref, o_ref):
    x = x_ref[...].astype(F32)
    y = x * lax.rsqrt(jnp.mean(x * x, axis=-1, keepdims=True) + NORM_EPS)
    o_ref[...] = (y * g_ref[...]).astype(o_ref.dtype)


def _rmsnorm(x, g, out_dtype, tm=256):
    t, d = x.shape
    return pl.pallas_call(
        _rmsnorm_kernel,
        grid=(t // tm,),
        in_specs=[pl.BlockSpec((tm, d), lambda i: (i, 0)), pl.BlockSpec((1, d), lambda i: (0, 0))],
        out_specs=pl.BlockSpec((tm, d), lambda i: (i, 0)),
        out_shape=jax.ShapeDtypeStruct((t, d), out_dtype),
        compiler_params=_params("arbitrary"),
        name="rmsnorm",
    )(x, g.reshape(1, d).astype(F32))


def _mm_nt_kernel(a_ref, bt_ref, o_ref):
    tn = bt_ref.shape[0]
    w = tn // 2 if tn % (2 * LANES) == 0 else tn
    for p in range(tn // w):
        bt = bt_ref[p * w:(p + 1) * w, :].astype(BF16)
        o_ref[:, p * w:(p + 1) * w] = lax.dot_general(a_ref[...], bt, _NT,
                                                      preferred_element_type=F32).astype(o_ref.dtype)


SUBLANES = 8


def _matmul_nt(a, bt, row_ranges, out_dtype, *, tm, tn, name):
    m, kdim = a.shape
    starts = []
    for first, n_rows in row_ranges:
        assert first % SUBLANES == 0 and n_rows % tn == 0, (first, n_rows)
        starts += [first + t * tn for t in range(n_rows // tn)]
    n = len(starts) * tn

    def bt_map(j, i):
        q = starts[-1] // SUBLANES
        for t in range(len(starts) - 2, -1, -1):
            q = jnp.where(j <= t, starts[t] // SUBLANES, q)
        return q * SUBLANES, 0

    return pl.pallas_call(
        _mm_nt_kernel,
        grid=(len(starts), m // tm),
        in_specs=[pl.BlockSpec((tm, kdim), lambda j, i: (i, 0)),
                  pl.BlockSpec((pl.Element(tn), pl.Element(kdim)), bt_map)],
        out_specs=pl.BlockSpec((tm, tn), lambda j, i: (i, j)),
        out_shape=jax.ShapeDtypeStruct((m, n), out_dtype),
        compiler_params=pltpu.CompilerParams(dimension_semantics=("arbitrary", "arbitrary"),
                                             vmem_limit_bytes=MATMUL_VMEM_LIMIT_BYTES),
        name=name,
    )(a, bt)


def _norm_mm_kernel(a_ref, g_ref, b_ref, o_ref):
    x = a_ref[...].astype(F32)
    y = x * lax.rsqrt(jnp.mean(x * x, axis=-1, keepdims=True) + NORM_EPS) * g_ref[...]
    o_ref[...] = jnp.dot(y.astype(BF16), b_ref[...], preferred_element_type=F32).astype(o_ref.dtype)


def _norm_matmul(a, a_col_block, g, b, *, tm, tn, name):
    m = a.shape[0]
    kdim, n = b.shape
    return pl.pallas_call(
        _norm_mm_kernel,
        grid=(m // tm, n // tn),
        in_specs=[pl.BlockSpec((tm, kdim), lambda i, j: (i, a_col_block)),
                  pl.BlockSpec((1, kdim), lambda i, j: (0, 0)),
                  pl.BlockSpec((kdim, tn), lambda i, j: (0, j))],
        out_specs=pl.BlockSpec((tm, tn), lambda i, j: (i, j)),
        out_shape=jax.ShapeDtypeStruct((m, n), BF16),
        compiler_params=_params("arbitrary", "arbitrary"),
        name=name,
    )(a, g.reshape(1, kdim).astype(F32), b)


def _outproj_kernel(oa_ref, ob_ref, w_ref, res_ref, h_ref):
    ka = oa_ref.shape[1]
    acc = jnp.dot(oa_ref[...], w_ref[:ka, :], preferred_element_type=F32)
    acc = acc + jnp.dot(ob_ref[...], w_ref[ka:, :], preferred_element_type=F32)
    h_ref[...] = acc + res_ref[...]


def _outproj(oa, ob, w, res, *, tm, tn):
    m, ka = oa.shape
    kb = ob.shape[1]
    n = w.shape[1]
    return pl.pallas_call(
        _outproj_kernel,
        grid=(m // tm, n // tn),
        in_specs=[pl.BlockSpec((tm, ka), lambda i, j: (i, 0)),
                  pl.BlockSpec((tm, kb), lambda i, j: (i, 0)),
                  pl.BlockSpec((ka + kb, tn), lambda i, j: (0, j)),
                  pl.BlockSpec((tm, tn), lambda i, j: (i, j))],
        out_specs=pl.BlockSpec((tm, tn), lambda i, j: (i, j)),
        out_shape=jax.ShapeDtypeStruct((m, n), F32),
        compiler_params=_params("arbitrary", "arbitrary"),
        name="outproj",
    )(oa, ob, w, res)


def _rope_tables(seq, rot_dim, period, scale=1.0):
    inv = ROPE_THETA ** (-jnp.arange(0, rot_dim, 2, dtype=F32) / rot_dim)
    ang = jnp.arange(seq, dtype=F32)[:, None] * inv[None, :]
    cos, sin = jnp.cos(ang), jnp.sin(ang)
    ones = jnp.ones((seq, period - rot_dim), F32)
    c = jnp.concatenate([cos, cos, ones], axis=1) * scale
    s = jnp.concatenate([-sin, sin, 0.0 * ones], axis=1) * scale
    reps = LANES // period
    return jnp.tile(c, (1, reps)), jnp.tile(s, (1, reps))


def _rope(x, c, s, half, period):
    lane = lax.broadcasted_iota(jnp.int32, x.shape, 1)
    first = (lane & (period - 1)) < half
    partner = jnp.where(first, pltpu.roll(x, LANES - half, axis=1), pltpu.roll(x, half, axis=1))
    return x * c + partner * s


def _softmax_pv(s, v, carry, mask=None):
    m, l, acc = carry
    if mask is not None:
        s = jnp.where(mask, s, MASKED)
    m_new = jnp.maximum(m, jnp.max(s, axis=1, keepdims=True))
    alpha = jnp.exp(m - m_new)
    p = jnp.exp(s - m_new)
    l = alpha * l + jnp.sum(p, axis=1, keepdims=True)
    acc = alpha * acc + jnp.dot(p.astype(BF16), v, preferred_element_type=F32)
    return m_new, l, acc


def _init_carry(rows, dv):
    return (jnp.full((rows, 1), MASKED, F32), jnp.zeros((rows, 1), F32), jnp.zeros((rows, dv), F32))


FLASH_SUB = 512


def _tri_mask(n):
    row = lax.broadcasted_iota(jnp.int32, (n, n), 0)
    col = lax.broadcasted_iota(jnp.int32, (n, n), 1)
    return row >= col


def _causal_sweep(logits, values, i, bq, n_chain, chains_per_sub, dv):
    sub = FLASH_SUB
    nsub = bq // sub

    def step(j, st):
        v = values(j * nsub, nsub)
        return tuple(_softmax_pv(logits(t, j * nsub, nsub), v, st[t]) for t in range(n_chain))

    st = list(lax.fori_loop(0, i, step, tuple(_init_carry(sub, dv) for _ in range(n_chain))))
    mask = _tri_mask(sub)
    for c in range(nsub):
        blk = i * nsub + c
        v = values(blk, 1)
        for t in range(n_chain):
            r = t // chains_per_sub
            if r >= c:
                st[t] = _softmax_pv(logits(t, blk, 1), v, st[t], mask if r == c else None)
    return st


def _key_rows(blk, nb):
    return pl.ds(pl.multiple_of(blk * FLASH_SUB, FLASH_SUB), nb * FLASH_SUB)


def _silu(g):
    return g * jax.nn.sigmoid(g)


def _mla_kernel(q_ref, kn_ref, kr_ref, v_ref, gate_ref, cq_ref, sq_ref, ck_ref, sk_ref, o_ref, k_scr, *, bq, scale):
    i = pl.program_id(2)
    half = MLA_ROPE // 2
    sub = FLASH_SUB

    @pl.when(i == 0)
    def _():
        k_scr[:, :LANES] = kn_ref[...]
        k_scr[:, LANES:] = _rope(kr_ref[...].astype(F32), ck_ref[...], sk_ref[...], half, LANES).astype(BF16)

    qs = []
    for r in range(bq // sub):
        rs = slice(r * sub, (r + 1) * sub)
        qn = (q_ref[rs, :LANES].astype(F32) * scale).astype(BF16)
        qr = _rope(q_ref[rs, LANES:].astype(F32), cq_ref[rs, :], sq_ref[rs, :], half, LANES).astype(BF16)
        qs.append(jnp.concatenate([qn, qr], axis=1))

    def logits(t, blk, nb):
        return lax.dot_general(qs[t], k_scr[_key_rows(blk, nb), :], _NT, preferred_element_type=F32)

    def values(blk, nb):
        return v_ref[_key_rows(blk, nb), :]

    st = _causal_sweep(logits, values, i, bq, bq // sub, 1, MLA_V)
    for r, (_, l, acc) in enumerate(st):
        rs = slice(r * sub, (r + 1) * sub)
        o_ref[rs, :] = (acc / l * _silu(gate_ref[rs, :].astype(F32))).astype(o_ref.dtype)


def _mla_attention(q, kv, lat, proj, tabs_q, tabs_k, batch, seq, *, bq):
    t = batch * seq
    nq = seq // bq
    kr_block = MLA_KV_RANK // LANES
    qrow = lambda b, h, i: (i, 0)
    full = lambda b, h, i: (0, 0)
    return pl.pallas_call(
        functools.partial(_mla_kernel, bq=bq, scale=(MLA_NOPE + MLA_ROPE) ** -0.5),
        grid=(batch, MLA_HEADS, nq),
        in_specs=[pl.BlockSpec((bq, 2 * LANES), lambda b, h, i: (b * nq + i, h)),
                  pl.BlockSpec((seq, LANES), lambda b, h, i: (b, h)),
                  pl.BlockSpec((seq, LANES), lambda b, h, i: (b, kr_block)),
                  pl.BlockSpec((seq, LANES), lambda b, h, i: (b, MLA_HEADS + h)),
                  pl.BlockSpec((bq, MLA_V), lambda b, h, i: (b * nq + i, h)),
                  pl.BlockSpec((bq, LANES), qrow), pl.BlockSpec((bq, LANES), qrow),
                  pl.BlockSpec((seq, LANES), full), pl.BlockSpec((seq, LANES), full)],
        out_specs=pl.BlockSpec((bq, MLA_V), lambda b, h, i: (b * nq + i, h)),
        out_shape=jax.ShapeDtypeStruct((t, MLA_WIDTH), BF16),
        scratch_shapes=[pltpu.VMEM((seq, 2 * LANES), BF16)],
        compiler_params=_params("arbitrary", "arbitrary", "arbitrary"),
        name="mla_attention",
    )(q, kv, lat, kv, proj, *tabs_q, *tabs_k)


def _diff_kernel(q_ref, k_ref, v_ref, gate_ref, lam_ref, g_ref, cq_ref, sq_ref, ck_ref, sk_ref, o_ref, k_scr,
                 *, bq, lambda_init):
    i = pl.program_id(2)
    half = DIFF_DIM // ROPE_FRACTION // 2
    sub = FLASH_SUB

    @pl.when(i == 0)
    def _():
        for c in range(2):
            sl = slice(c * LANES, (c + 1) * LANES)
            k_scr[:, sl] = _rope(k_ref[:, sl].astype(F32), ck_ref[...], sk_ref[...], half, LANES).astype(BF16)

    qs = []
    for r in range(bq // sub):
        rs = slice(r * sub, (r + 1) * sub)
        for c in range(2):
            x = q_ref[rs, c * LANES:(c + 1) * LANES].astype(F32)
            qs.append(_rope(x, cq_ref[rs, :], sq_ref[rs, :], half, LANES).astype(BF16))

    def logits(t, blk, nb):
        c = t % 2
        return lax.dot_general(qs[t], k_scr[_key_rows(blk, nb), c * LANES:(c + 1) * LANES], _NT,
                               preferred_element_type=F32)

    def values(blk, nb):
        return v_ref[_key_rows(blk, nb), :]

    st = _causal_sweep(logits, values, i, bq, 2 * (bq // sub), 2, 2 * DIFF_DIM)

    lp = lam_ref[...]
    lam = (jnp.exp(jnp.sum(lp[0:1] * lp[1:2], axis=1, keepdims=True))
           - jnp.exp(jnp.sum(lp[2:3] * lp[3:4], axis=1, keepdims=True)) + lambda_init)
    for r in range(bq // sub):
        rs = slice(r * sub, (r + 1) * sub)
        (_, l0, acc0), (_, l1, acc1) = st[2 * r], st[2 * r + 1]
        o = acc0 / l0 - lam * (acc1 / l1)
        o = o * lax.rsqrt(jnp.mean(o * o, axis=-1, keepdims=True) + NORM_EPS) * g_ref[...]
        o_ref[rs, :] = (o * (1.0 - lambda_init) * _silu(gate_ref[rs, :].astype(F32))).astype(o_ref.dtype)


def _diff_attention(proj, lam_params, subln_g, tabs_q, tabs_k, batch, seq, lambda_init, *, bq):
    t = batch * seq
    nq = seq // bq
    w = 2 * DIFF_DIM
    q0 = MLA_WIDTH // w
    k0, v0, g0 = q0 + DIFF_HEADS, q0 + 2 * DIFF_HEADS, q0 + 3 * DIFF_HEADS
    qrow = lambda b, h, i: (i, 0)
    full = lambda b, h, i: (0, 0)
    return pl.pallas_call(
        functools.partial(_diff_kernel, bq=bq, lambda_init=lambda_init),
        grid=(batch, DIFF_HEADS, nq),
        in_specs=[pl.BlockSpec((bq, w), lambda b, h, i: (b * nq + i, q0 + h)),
                  pl.BlockSpec((seq, w), lambda b, h, i: (b, k0 + h)),
                  pl.BlockSpec((seq, w), lambda b, h, i: (b, v0 + h)),
                  pl.BlockSpec((bq, w), lambda b, h, i: (b * nq + i, g0 + h)),
                  pl.BlockSpec((4, DIFF_DIM), full),
                  pl.BlockSpec((1, w), full),
                  pl.BlockSpec((bq, LANES), qrow), pl.BlockSpec((bq, LANES), qrow),
                  pl.BlockSpec((seq, LANES), full), pl.BlockSpec((seq, LANES), full)],
        out_specs=pl.BlockSpec((bq, w), lambda b, h, i: (b * nq + i, h)),
        out_shape=jax.ShapeDtypeStruct((t, DIFF_WIDTH), BF16),
        scratch_shapes=[pltpu.VMEM((seq, w), BF16)],
        compiler_params=_params("arbitrary", "arbitrary", "arbitrary"),
        name="diff_attention",
    )(proj, proj, proj, proj, lam_params.astype(F32), subln_g.reshape(1, w).astype(F32), *tabs_q, *tabs_k)


def _fox_kernel(q_ref, k_ref, v_ref, gate_ref, cum_ref, cumt_ref, o_ref, *, bq, scale):
    h = pl.program_id(1)
    i = pl.program_id(2)
    sub = FLASH_SUB
    lane = lax.broadcasted_iota(jnp.int32, (sub, LANES), 1)
    qs, cqs = [], []
    for r in range(bq // sub):
        rs = slice(r * sub, (r + 1) * sub)
        qs.append((q_ref[rs, :].astype(F32) * scale).astype(BF16))
        cqs.append(jnp.sum(jnp.where(lane == FOX_F_LANE + h, cum_ref[rs, :], 0.0), axis=1, keepdims=True))

    def logits(t, blk, nb):
        s = lax.dot_general(qs[t], k_ref[_key_rows(blk, nb), :], _NT, preferred_element_type=F32)
        ck = jnp.concatenate([cumt_ref[0, pl.ds(blk + n, 1), :] for n in range(nb)], axis=1)
        return s + (cqs[t] - ck)

    def values(blk, nb):
        return v_ref[_key_rows(blk, nb), :]

    st = _causal_sweep(logits, values, i, bq, bq // sub, 1, FOX_DIM)
    for r, (_, l, acc) in enumerate(st):
        rs = slice(r * sub, (r + 1) * sub)
        o_ref[rs, :] = (acc / l * _silu(gate_ref[rs, :].astype(F32))).astype(o_ref.dtype)


def _fox_attention(proj, cum, cum_t, batch, seq, *, bq):
    t = batch * seq
    nq = seq // bq
    q0 = DSA_WIDTH // LANES
    k0, v0, g0 = q0 + FOX_HEADS, q0 + 2 * FOX_HEADS, q0 + 3 * FOX_HEADS
    return pl.pallas_call(
        functools.partial(_fox_kernel, bq=bq, scale=FOX_DIM ** -0.5),
        grid=(batch, FOX_HEADS, nq),
        in_specs=[pl.BlockSpec((bq, LANES), lambda b, h, i: (b * nq + i, q0 + h)),
                  pl.BlockSpec((seq, LANES), lambda b, h, i: (b, k0 + h)),
                  pl.BlockSpec((seq, LANES), lambda b, h, i: (b, v0 + h)),
                  pl.BlockSpec((bq, LANES), lambda b, h, i: (b * nq + i, g0 + h)),
                  pl.BlockSpec((bq, LANES), lambda b, h, i: (b * nq + i, 0)),
                  pl.BlockSpec((1, seq // FLASH_SUB, FLASH_SUB), lambda b, h, i: (b * FOX_HEADS + h, 0, 0))],
        out_specs=pl.BlockSpec((bq, FOX_DIM), lambda b, h, i: (b * nq + i, h)),
        out_shape=jax.ShapeDtypeStruct((t, FOX_WIDTH), BF16),
        compiler_params=_params("arbitrary", "arbitrary", "arbitrary"),
        name="fox_attention",
    )(proj, proj, proj, proj, cum, cum_t)


def _logf_cumsum_kernel(f_ref, b_ref, o_ref, carry_ref, *, tb):
    @pl.when(pl.program_id(1) == 0)
    def _():
        carry_ref[...] = jnp.zeros_like(carry_ref)

    x = f_ref[...] + b_ref[...]
    logf = jnp.minimum(x, 0.0) - jnp.log1p(jnp.exp(-jnp.abs(x)))
    row = lax.broadcasted_iota(jnp.int32, (tb, tb), 0)
    col = lax.broadcasted_iota(jnp.int32, (tb, tb), 1)
    tri = jnp.where(row >= col, 1.0, 0.0).astype(BF16)
    hi = logf.astype(BF16)
    r1 = logf - hi.astype(F32)
    mid = r1.astype(BF16)
    lo = (r1 - mid.astype(F32)).astype(BF16)
    cum = (jnp.dot(tri, hi, preferred_element_type=F32) + jnp.dot(tri, mid, preferred_element_type=F32)
           + jnp.dot(tri, lo, preferred_element_type=F32)) + carry_ref[...]
    o_ref[...] = cum
    carry_ref[...] = cum[tb - 1:tb, :]


def _logf_cumsum(small, forget_bias, batch, seq, *, tb=512):
    t = batch * seq
    nb = seq // tb
    bias = jnp.zeros((1, LANES), F32).at[0, FOX_F_LANE:FOX_F_LANE + FOX_HEADS].set(forget_bias.astype(F32))
    return pl.pallas_call(
        functools.partial(_logf_cumsum_kernel, tb=tb),
        grid=(batch, nb),
        in_specs=[pl.BlockSpec((tb, LANES), lambda b, i: (b * nb + i, 2)),
                  pl.BlockSpec((1, LANES), lambda b, i: (0, 0))],
        out_specs=pl.BlockSpec((tb, LANES), lambda b, i: (b * nb + i, 0)),
        out_shape=jax.ShapeDtypeStruct((t, LANES), F32),
        scratch_shapes=[pltpu.VMEM((1, LANES), F32)],
        compiler_params=_params("arbitrary", "arbitrary"),
        name="logf_cumsum",
    )(small, bias)


DSA_CK = 512
DSA_BQ = 256
DSA_GROUP = 2
DSA_MAX_ITERS = 640
DSA_PROBES_PER_CHECK = 4


def _row_total(x):
    return jnp.broadcast_to(jnp.sum(x, axis=1, keepdims=True), x.shape)


def _tile_lanes(x, n):
    return jnp.concatenate([x] * n, axis=1)


def _dsa_kernel(qc_ref, qi_ref, wq_ref, kv_ref, gate_ref, cdq_ref, sdq_ref, cdk_ref, sdk_ref, ciq_ref, siq_ref,
                cik_ref, sik_ref, o_ref, kc_scr, vc_scr, ki_scr, idx_scr, q_scr, qi_scr, w_scr, *, seq, topk):
    i = pl.program_id(1)
    bq, ck = DSA_BQ, DSA_CK
    dhalf = DSA_DIM // ROPE_FRACTION // 2
    ihalf = IDX_DIM // ROPE_FRACTION // 2
    rows = DSA_HEADS * bq
    lane = lax.broadcasted_iota(jnp.int32, (bq, LANES), 1)

    @pl.when(i == 0)
    def _():
        tb = 512

        def prep(r, _):
            off = pl.multiple_of(r * tb, tb)
            sl = pl.ds(off, tb)
            kc_scr[sl, :] = _rope(kv_ref[sl, 0:LANES], cdk_ref[sl, :], sdk_ref[sl, :], dhalf, LANES).astype(BF16)
            vc_scr[sl, :LANES] = kv_ref[sl, LANES:2 * LANES].astype(BF16)
            vc_scr[sl, LANES:] = jnp.ones((tb, LANES), BF16)
            lane_t = lax.broadcasted_iota(jnp.int32, (tb, LANES), 1)
            ki = jnp.where(lane_t < IDX_DIM, kv_ref[sl, 2 * LANES:3 * LANES], 0.0)
            ki = _rope(ki, cik_ref[sl, :], sik_ref[sl, :], ihalf, IDX_DIM)
            ki_scr[sl, :] = (ki + pltpu.roll(ki, IDX_DIM, axis=1)).astype(BF16)
            return 0

        lax.fori_loop(0, seq // tb, prep, 0)

    for h in range(DSA_HEADS):
        xh = qc_ref[:, h * LANES:(h + 1) * LANES].astype(F32)
        q_scr[h * bq:(h + 1) * bq, :] = _rope(xh, cdq_ref[...], sdq_ref[...], dhalf, LANES).astype(BF16)
    for p in range(IDX_HEADS // 2):
        xp = _rope(qi_ref[:, p * LANES:(p + 1) * LANES].astype(F32), ciq_ref[...], siq_ref[...], ihalf, IDX_DIM)
        qi_scr[(2 * p) * bq:(2 * p + 1) * bq, :] = jnp.where(lane < IDX_DIM, xp, 0.0).astype(BF16)
        qi_scr[(2 * p + 1) * bq:(2 * p + 2) * bq, :] = jnp.where(lane >= IDX_DIM, xp, 0.0).astype(BF16)
    w = wq_ref[:, 2 * LANES:3 * LANES] * ((IDX_HEADS ** -0.5) * (IDX_DIM ** -0.5))
    for h in range(IDX_HEADS):
        w_scr[h * bq:(h + 1) * bq, :] = _row_total(jnp.where(lane == IDX_W_LANE + h, w, 0.0))

    nch = i // (ck // bq) + 1
    qpos = i * bq + lax.broadcasted_iota(jnp.int32, (bq, LANES), 0)
    inf = jnp.full((bq, LANES), jnp.inf, F32)

    def idx_step(j, carry):
        mn, mx = carry
        off = pl.multiple_of(j * ck, ck)
        r = lax.dot_general(qi_scr[...], ki_scr[pl.ds(off, ck), :], _NT, preferred_element_type=F32)
        halves = []
        for c in range(ck // LANES):
            sc = None
            for h in range(IDX_HEADS):
                term = jnp.maximum(r[h * bq:(h + 1) * bq, c * LANES:(c + 1) * LANES], 0.0) * w_scr[h * bq:(h + 1) * bq, :]
                sc = term if sc is None else sc + term
            valid = (off + c * LANES + lane) <= qpos
            halves.append(jnp.where(valid, sc, -inf))
            mn = jnp.minimum(mn, jnp.where(valid, sc, inf))
            mx = jnp.maximum(mx, jnp.where(valid, sc, -inf))
        idx_scr[j] = jnp.concatenate(halves, axis=1)
        return mn, mx

    mn, mx = lax.fori_loop(0, nch, idx_step, (inf, -inf))
    mn = jnp.broadcast_to(jnp.min(mn, axis=1, keepdims=True), mn.shape)
    mx = jnp.broadcast_to(jnp.max(mx, axis=1, keepdims=True), mx.shape)

    def count_ge(t):
        nslab = bq // LANES

        def body(j, cs):
            out = []
            for u in range(nslab):
                rs = slice(u * LANES, (u + 1) * LANES)
                x = idx_scr[j, rs, :]
                c = cs[u]
                for half in range(ck // LANES):
                    c = c + jnp.where(x[:, half * LANES:(half + 1) * LANES] >= t[rs], 1.0, 0.0)
                out.append(c)
            return tuple(out)

        cs = lax.fori_loop(0, nch, body, tuple(jnp.zeros((LANES, LANES), F32) for _ in range(nslab)))
        return _row_total(jnp.concatenate(cs, axis=0))

    kf = float(topk)
    nvalid = (qpos + 1).astype(F32)
    c_mx = count_ge(mx)
    few = nvalid <= kf
    top_tie = (~few) & (c_mx >= kf)
    lo = jnp.where(top_tie, mx, mn)
    c_lo = jnp.where(top_tie, c_mx, nvalid)
    hi = jnp.where(top_tie, inf, mx)
    c_hi = jnp.where(top_tie, 0.0, c_mx)
    done = jnp.where(few | top_tie | (c_lo == kf), 1.0, 0.0)

    def search_cond(st):
        it, _, _, _, _, done = st
        return jnp.logical_and(it < DSA_MAX_ITERS, jnp.min(done) < 0.5)

    def probe(st):
        lo, hi, c_lo, c_hi, done = st
        mid = 0.5 * lo + 0.5 * hi
        adjacent = (mid <= lo) | (mid >= hi)
        c = count_ge(mid)
        ge = c >= kf
        upd = (done < 0.5) & (~adjacent)
        lo = jnp.where(upd & ge, mid, lo)
        c_lo = jnp.where(upd & ge, c, c_lo)
        hi = jnp.where(upd & (~ge), mid, hi)
        c_hi = jnp.where(upd & (~ge), c, c_hi)
        done = jnp.where(adjacent | (c_lo == kf), 1.0, done)
        return lo, hi, c_lo, c_hi, done

    def search_body(st):
        it, rest = st[0], st[1:]
        for _ in range(DSA_PROBES_PER_CHECK):
            rest = probe(rest)
        return (it + DSA_PROBES_PER_CHECK,) + tuple(rest)

    _, lo, hi, c_lo, c_hi, _ = lax.while_loop(search_cond, search_body, (jnp.int32(0), lo, hi, c_lo, c_hi, done))

    nl = ck // LANES
    lo_w = _tile_lanes(lo, nl)

    def write_plain():
        def body(j, _):
            idx_scr[j] = jnp.where(idx_scr[j] >= lo_w, 0.0, MASKED)
            return 0
        lax.fori_loop(0, nch, body, 0)

    def write_ties():
        hi_w = _tile_lanes(hi, nl)
        need = _tile_lanes(kf - c_hi, nl)
        urow = lax.broadcasted_iota(jnp.int32, (ck, ck), 0)
        ucol = lax.broadcasted_iota(jnp.int32, (ck, ck), 1)
        upper = jnp.where(urow < ucol, 1.0, 0.0).astype(BF16)

        def body(j, before):
            x = idx_scr[j]
            above = x >= hi_w
            eq = jnp.where((x >= lo_w) & (~above), 1.0, 0.0)
            prefix = jnp.dot(eq.astype(BF16), upper, preferred_element_type=F32) + _tile_lanes(before, nl)
            sel = above | ((eq > 0.5) & (prefix < need))
            idx_scr[j] = jnp.where(sel, 0.0, MASKED)
            tot = eq[:, :LANES]
            for c in range(1, nl):
                tot = tot + eq[:, c * LANES:(c + 1) * LANES]
            return before + _row_total(tot)

        lax.fori_loop(0, nch, body, jnp.zeros((bq, LANES), F32))

    lax.cond(jnp.max(c_lo) > kf, write_ties, write_plain)

    grp = DSA_GROUP * bq
    ngrp = rows // grp

    def att_step(j, carry):
        off = pl.multiple_of(j * ck, ck)
        k = kc_scr[pl.ds(off, ck), :]
        v = vc_scr[pl.ds(off, ck), :]
        bias = jnp.concatenate([idx_scr[j]] * DSA_GROUP, axis=0)
        out = []
        for g in range(ngrp):
            m, acc = carry[g]
            s = lax.dot_general(q_scr[g * grp:(g + 1) * grp, :], k, _NT, preferred_element_type=F32)
            parts = [s[:, c * LANES:(c + 1) * LANES] + bias[:, c * LANES:(c + 1) * LANES] for c in range(nl)]
            mc = parts[0]
            for c in range(1, nl):
                mc = jnp.maximum(mc, parts[c])
            m_new = jnp.maximum(m, jnp.max(mc, axis=1, keepdims=True))
            alpha = jnp.exp(m - m_new)
            p = jnp.concatenate([jnp.exp(x - m_new) for x in parts], axis=1).astype(BF16)
            acc = _tile_lanes(alpha, 2) * acc + jnp.dot(p, v, preferred_element_type=F32)
            out.append((m_new, acc))
        return tuple(out)

    init = tuple((jnp.full((grp, LANES), MASKED, F32), jnp.zeros((grp, 2 * LANES), F32)) for _ in range(ngrp))
    res = lax.fori_loop(0, nch, att_step, init)
    for g in range(ngrp):
        _, acc = res[g]
        o = acc[:, :LANES] / acc[:, LANES:]
        for t in range(DSA_GROUP):
            hl = slice((g * DSA_GROUP + t) * LANES, (g * DSA_GROUP + t + 1) * LANES)
            o_ref[:, hl] = (o[t * bq:(t + 1) * bq] * _silu(gate_ref[:, hl].astype(F32))).astype(o_ref.dtype)


def _dsa_attention(proj, small, dsa_tabs_q, dsa_tabs_k, idx_tabs, batch, seq):
    t = batch * seq
    bq = DSA_BQ
    nq = seq // bq
    topk = min(TOPK_MAX, seq // 4)
    rows = DSA_HEADS * bq
    idx_w = IDX_HEADS * IDX_DIM
    c_dsa_q = 2 * DSA_WIDTH + 3 * FOX_WIDTH
    qrow = lambda b, i: (i, 0)
    full = lambda b, i: (0, 0)
    return pl.pallas_call(
        functools.partial(_dsa_kernel, seq=seq, topk=topk),
        grid=(batch, nq),
        in_specs=[pl.BlockSpec((bq, DSA_WIDTH), lambda b, i: (b * nq + i, c_dsa_q // DSA_WIDTH)),
                  pl.BlockSpec((bq, idx_w), lambda b, i: (b * nq + i, (c_dsa_q + DSA_WIDTH) // idx_w)),
                  pl.BlockSpec((bq, 3 * LANES), lambda b, i: (b * nq + i, 0)),
                  pl.BlockSpec((seq, 3 * LANES), lambda b, i: (b, 0)),
                  pl.BlockSpec((bq, DSA_WIDTH), lambda b, i: (b * nq + i, 0)),
                  pl.BlockSpec((bq, LANES), qrow), pl.BlockSpec((bq, LANES), qrow),
                  pl.BlockSpec((seq, LANES), full), pl.BlockSpec((seq, LANES), full),
                  pl.BlockSpec((bq, LANES), qrow), pl.BlockSpec((bq, LANES), qrow),
                  pl.BlockSpec((seq, LANES), full), pl.BlockSpec((seq, LANES), full)],
        out_specs=pl.BlockSpec((bq, DSA_WIDTH), lambda b, i: (b * nq + i, 0)),
        out_shape=jax.ShapeDtypeStruct((t, DSA_WIDTH), BF16),
        scratch_shapes=[pltpu.VMEM((seq, LANES), BF16), pltpu.VMEM((seq, 2 * LANES), BF16),
                        pltpu.VMEM((seq, LANES), BF16), pltpu.VMEM((seq // DSA_CK, bq, DSA_CK), F32),
                        pltpu.VMEM((rows, LANES), BF16), pltpu.VMEM((rows, LANES), BF16),
                        pltpu.VMEM((rows, LANES), F32)],
        compiler_params=_params("arbitrary", "arbitrary"),
        name="dsa_attention",
    )(proj, proj, small, small, proj, *dsa_tabs_q, *dsa_tabs_k, *idx_tabs, *idx_tabs)


def _even_layer(x2, batch, seq, norm_g, w_in, q_norm_g, w_uq, kv_norm_g, w_ukv, diff_lambda, subln_g, w_out,
                lambda_init, bq):
    wt = w_in.T
    n_lat = MLA_Q_RANK + MLA_KV_RANK + MLA_ROPE
    n_big = 2 * MLA_WIDTH + 3 * DIFF_WIDTH
    n_kv_tile = MLA_KV_RANK + LANES
    uq = w_uq.astype(BF16).reshape(MLA_Q_RANK, MLA_HEADS, MLA_NOPE + MLA_ROPE)
    uq = jnp.concatenate([uq, jnp.zeros((MLA_Q_RANK, MLA_HEADS, 2 * LANES - MLA_NOPE - MLA_ROPE), BF16)], axis=2)
    uq = uq.reshape(MLA_Q_RANK, MLA_HEADS * 2 * LANES)
    ukv = w_ukv.astype(BF16).reshape(MLA_KV_RANK, MLA_HEADS, MLA_NOPE + MLA_V)
    ukv = jnp.concatenate([ukv[:, :, :MLA_NOPE].reshape(MLA_KV_RANK, -1), ukv[:, :, MLA_NOPE:].reshape(MLA_KV_RANK, -1)],
                          axis=1)

    h = _rmsnorm(x2, norm_g, BF16)
    proj = _matmul_nt(h, wt, [(n_lat, n_big)], BF16, tm=1024, tn=1024, name="even_inproj")
    lat_q = _matmul_nt(h, wt, [(0, MLA_Q_RANK)], BF16, tm=1024, tn=MLA_Q_RANK, name="even_latent_q")
    lat_kv = _matmul_nt(h, wt, [(MLA_Q_RANK, n_kv_tile)], BF16, tm=1024, tn=n_kv_tile, name="even_latent_kv")
    q = _norm_matmul(lat_q, 0, q_norm_g, uq, tm=1024, tn=1024, name="mla_q_up")
    kv = _norm_matmul(lat_kv, 0, kv_norm_g, ukv, tm=1024, tn=1024, name="mla_kv_up")

    mla_scale = (MLA_NOPE + MLA_ROPE) ** -0.5
    o_a = _mla_attention(q, kv, lat_kv, proj, _rope_tables(seq, MLA_ROPE, LANES, mla_scale),
                         _rope_tables(seq, MLA_ROPE, LANES), batch, seq, bq=bq)
    rot = DIFF_DIM // ROPE_FRACTION
    o_b = _diff_attention(proj, diff_lambda, subln_g, _rope_tables(seq, rot, LANES, DIFF_DIM ** -0.5),
                          _rope_tables(seq, rot, LANES), batch, seq, lambda_init, bq=bq)
    return _outproj(o_a, o_b, w_out.astype(BF16), x2, tm=1024, tn=1024)


def _odd_layer(x2, batch, seq, norm_g, w_in, forget_bias, w_out, bq):
    wt = w_in.T
    c_dsa_k = DSA_WIDTH
    c_idx_q = c_dsa_k + 2 * DSA_DIM
    c_idx_k = c_idx_q + IDX_HEADS * IDX_DIM
    c_gate_c = c_idx_k + IDX_DIM + IDX_HEADS
    c_fox_f = c_gate_c + DSA_WIDTH + 3 * FOX_WIDTH
    c_gate_d = c_fox_f + FOX_HEADS
    big_rows = [(c_gate_c, DSA_WIDTH + 3 * FOX_WIDTH), (c_gate_d, FOX_WIDTH), (0, DSA_WIDTH),
                (c_idx_q, IDX_HEADS * IDX_DIM)]
    n_small = 2 * DSA_DIM + IDX_DIM + IDX_HEADS + FOX_HEADS
    w_small = jnp.concatenate([wt[c_dsa_k:c_dsa_k + 2 * DSA_DIM], wt[c_idx_k:c_idx_k + IDX_DIM + IDX_HEADS],
                               wt[c_fox_f:c_fox_f + FOX_HEADS],
                               jnp.zeros((3 * LANES - n_small, wt.shape[1]), wt.dtype)], axis=0)

    h = _rmsnorm(x2, norm_g, BF16)
    proj = _matmul_nt(h, wt, big_rows, BF16, tm=1024, tn=1024, name="odd_inproj")
    small = _matmul_nt(h, w_small, [(0, 3 * LANES)], F32, tm=1024, tn=3 * LANES, name="odd_small")

    rot = DSA_DIM // ROPE_FRACTION
    o_c = _dsa_attention(proj, small, _rope_tables(seq, rot, LANES, DSA_DIM ** -0.5), _rope_tables(seq, rot, LANES),
                         _rope_tables(seq, IDX_DIM // ROPE_FRACTION, IDX_DIM), batch, seq)
    cum = _logf_cumsum(small, forget_bias, batch, seq)
    cum_t = cum[:, FOX_F_LANE:FOX_F_LANE + FOX_HEADS].reshape(batch, seq, FOX_HEADS).transpose(0, 2, 1)
    cum_t = cum_t.reshape(batch * FOX_HEADS, seq // FLASH_SUB, FLASH_SUB)
    o_d = _fox_attention(proj, cum, cum_t, batch, seq, bq=bq)
    return _outproj(o_c, o_d, w_out.astype(BF16), x2, tm=1024, tn=1024)


def kernel(x, even_norm, even_w_in, mla_q_norm, mla_w_uq, mla_kv_norm, mla_w_ukv, diff_lambda, diff_subln, even_w_out, odd_norm, odd_w_in, fox_forget_bias, odd_w_out, final_norm):
    batch, seq, d = x.shape
    bq = min(1024, seq)
    h = x.reshape(batch * seq, d)
    depth = even_norm.shape[0] + odd_norm.shape[0]
    for layer in range(depth):
        i = layer // 2
        if layer % 2 == 0:
            lambda_init = 0.8 - 0.6 * math.exp(-0.3 * layer)
            h = _even_layer(h, batch, seq, even_norm[i], even_w_in[i], mla_q_norm[i], mla_w_uq[i], mla_kv_norm[i],
                            mla_w_ukv[i], diff_lambda[i], diff_subln[i], even_w_out[i], lambda_init, bq)
        else:
            h = _odd_layer(h, batch, seq, odd_norm[i], odd_w_in[i], fox_forget_bias[i], odd_w_out[i], bq)
    return _rmsnorm(h, final_norm, x.dtype).reshape(batch, seq, d)
```

```python
import functools
import math

import jax
import jax.numpy as jnp
from jax import lax
from jax.experimental import pallas as pl
from jax.experimental.pallas import tpu as pltpu

F32 = jnp.float32
BF16 = jnp.bfloat16

ROPE_THETA = 500000.0
NORM_EPS = 1e-6
ROPE_FRACTION = 4

MLA_HEADS, MLA_NOPE, MLA_ROPE, MLA_V = 16, 128, 64, 128
MLA_Q_RANK, MLA_KV_RANK = 1024, 512
DIFF_HEADS, DIFF_DIM = 8, 128
DSA_HEADS, DSA_DIM = 16, 128
IDX_HEADS, IDX_DIM = 16, 64
TOPK_MAX = 256
FOX_HEADS, FOX_DIM = 16, 128

MLA_WIDTH = MLA_HEADS * MLA_V
DIFF_WIDTH = DIFF_HEADS * 2 * DIFF_DIM
DSA_WIDTH = DSA_HEADS * DSA_DIM
FOX_WIDTH = FOX_HEADS * FOX_DIM

LANES = 128
VMEM_LIMIT_BYTES = 56 * 1024 * 1024
MATMUL_VMEM_LIMIT_BYTES = 58 * 1024 * 1024
MASKED = -1e30

IDX_W_LANE = IDX_DIM
FOX_F_LANE = IDX_DIM + IDX_HEADS

_NT = (((1,), (1,)), ((), ()))


def _params(*sem):
    return pltpu.CompilerParams(dimension_semantics=sem, vmem_limit_bytes=VMEM_LIMIT_BYTES)


def _rmsnorm_kernel(x_ref, g_ref, o_ref):
    x = x_ref[...].astype(F32)
    y = x * lax.rsqrt(jnp.mean(x * x, axis=-1, keepdims=True) + NORM_EPS)
    o_ref[...] = (y * g_ref[...]).astype(o_ref.dtype)


def _rmsnorm(x, g, out_dtype, tm=256):
    t, d = x.shape
    return pl.pallas_call(
        _rmsnorm_kernel,
        grid=(t // tm,),
        in_specs=[pl.BlockSpec((tm, d), lambda i: (i, 0)), pl.BlockSpec((1, d), lambda i: (0, 0))],
        out_specs=pl.BlockSpec((tm, d), lambda i: (i, 0)),
        out_shape=jax.ShapeDtypeStruct((t, d), out_dtype),
        compiler_params=_params("arbitrary"),
        name="rmsnorm",
    )(x, g.reshape(1, d).astype(F32))


def _mm_nt_kernel(a_ref, bt_ref, o_ref):
    tn = bt_ref.shape[0]
    w = tn // 2 if tn % (2 * LANES) == 0 else tn
    for p in range(tn // w):
        bt = bt_ref[p * w:(p + 1) * w, :].astype(BF16)
        o_ref[:, p * w:(p + 1) * w] = lax.dot_general(a_ref[...], bt, _NT,
                                                      preferred_element_type=F32).astype(o_ref.dtype)


SUBLANES = 8


def _matmul_nt(a, bt, row_ranges, out_dtype, *, tm, tn, name):
    m, kdim = a.shape
    starts = []
    for first, n_rows in row_ranges:
        assert first % SUBLANES == 0 and n_rows % tn == 0, (first, n_rows)
        starts += [first + t * tn for t in range(n_rows // tn)]
    n = len(starts) * tn

    def bt_map(j, i):
        q = starts[-1] // SUBLANES
        for t in range(len(starts) - 2, -1, -1):
            q = jnp.where(j <= t, starts[t] // SUBLANES, q)
        return q * SUBLANES, 0

    return pl.pallas_call(
        _mm_nt_kernel,
        grid=(len(starts), m // tm),
        in_specs=[pl.BlockSpec((tm, kdim), lambda j, i: (i, 0)),
                  pl.BlockSpec((pl.Element(tn), pl.Element(kdim)), bt_map)],
        out_specs=pl.BlockSpec((tm, tn), lambda j, i: (i, j)),
        out_shape=jax.ShapeDtypeStruct((m, n), out_dtype),
        compiler_params=pltpu.CompilerParams(dimension_semantics=("arbitrary", "arbitrary"),
                                             vmem_limit_bytes=MATMUL_VMEM_LIMIT_BYTES),
        name=name,
    )(a, bt)


def _norm_mm_kernel(a_ref, g_ref, b_ref, o_ref):
    x = a_ref[...].astype(F32)
    y = x * lax.rsqrt(jnp.mean(x * x, axis=-1, keepdims=True) + NORM_EPS) * g_ref[...]
    o_ref[...] = jnp.dot(y.astype(BF16), b_ref[...], preferred_element_type=F32).astype(o_ref.dtype)


def _norm_matmul(a, a_col_block, g, b, *, tm, tn, name):
    m = a.shape[0]
    kdim, n = b.shape
    return pl.pallas_call(
        _norm_mm_kernel,
        grid=(m // tm, n // tn),
        in_specs=[pl.BlockSpec((tm, kdim), lambda i, j: (i, a_col_block)),
                  pl.BlockSpec((1, kdim), lambda i, j: (0, 0)),
                  pl.BlockSpec((kdim, tn), lambda i, j: (0, j))],
        out_specs=pl.BlockSpec((tm, tn), lambda i, j: (i, j)),
        out_shape=jax.ShapeDtypeStruct((m, n), BF16),
        compiler_params=_params("arbitrary", "arbitrary"),
        name=name,
    )(a, g.reshape(1, kdim).astype(F32), b)


def _outproj_kernel(oa_ref, ob_ref, w_ref, res_ref, h_ref):
    ka = oa_ref.shape[1]
    acc = jnp.dot(oa_ref[...], w_ref[:ka, :], preferred_element_type=F32)
    acc = acc + jnp.dot(ob_ref[...], w_ref[ka:, :], preferred_element_type=F32)
    h_ref[...] = acc + res_ref[...]


def _outproj(oa, ob, w, res, *, tm, tn):
    m, ka = oa.shape
    kb = ob.shape[1]
    n = w.shape[1]
    return pl.pallas_call(
        _outproj_kernel,
        grid=(m // tm, n // tn),
        in_specs=[pl.BlockSpec((tm, ka), lambda i, j: (i, 0)),
                  pl.BlockSpec((tm, kb), lambda i, j: (i, 0)),
                  pl.BlockSpec((ka + kb, tn), lambda i, j: (0, j)),
                  pl.BlockSpec((tm, tn), lambda i, j: (i, j))],
        out_specs=pl.BlockSpec((tm, tn), lambda i, j: (i, j)),
        out_shape=jax.ShapeDtypeStruct((m, n), F32),
        compiler_params=_params("arbitrary", "arbitrary"),
        name="outproj",
    )(oa, ob, w, res)


def _rope_tables(seq, rot_dim, period, scale=1.0):
    inv = ROPE_THETA ** (-jnp.arange(0, rot_dim, 2, dtype=F32) / rot_dim)
    ang = jnp.arange(seq, dtype=F32)[:, None] * inv[None, :]
    cos, sin = jnp.cos(ang), jnp.sin(ang)
    ones = jnp.ones((seq, period - rot_dim), F32)
    c = jnp.concatenate([cos, cos, ones], axis=1) * scale
    s = jnp.concatenate([-sin, sin, 0.0 * ones], axis=1) * scale
    reps = LANES // period
    return jnp.tile(c, (1, reps)), jnp.tile(s, (1, reps))


def _rope(x, c, s, half, period):
    lane = lax.broadcasted_iota(jnp.int32, x.shape, 1)
    first = (lane & (period - 1)) < half
    partner = jnp.where(first, pltpu.roll(x, LANES - half, axis=1), pltpu.roll(x, half, axis=1))
    return x * c + partner * s


def _softmax_pv(s, v, carry, mask=None):
    m, l, acc = carry
    if mask is not None:
        s = jnp.where(mask, s, MASKED)
    m_new = jnp.maximum(m, jnp.max(s, axis=1, keepdims=True))
    alpha = jnp.exp(m - m_new)
    p = jnp.exp(s - m_new)
    l = alpha * l + jnp.sum(p, axis=1, keepdims=True)
    acc = alpha * acc + jnp.dot(p.astype(BF16), v, preferred_element_type=F32)
    return m_new, l, acc


def _init_carry(rows, dv):
    return (jnp.full((rows, 1), MASKED, F32), jnp.zeros((rows, 1), F32), jnp.zeros((rows, dv), F32))


FLASH_SUB = 512
FLASH_BQ = 2048
DIFF_BQ = 1024


def _tri_mask(n):
    row = lax.broadcasted_iota(jnp.int32, (n, n), 0)
    col = lax.broadcasted_iota(jnp.int32, (n, n), 1)
    return row >= col


def _causal_sweep(logits, values, i, bq, n_chain, chains_per_sub, dv):
    sub = FLASH_SUB
    nsub = bq // sub

    def step(j, st):
        v = values(j * nsub, nsub)
        return tuple(_softmax_pv(logits(t, j * nsub, nsub), v, st[t]) for t in range(n_chain))

    st = list(lax.fori_loop(0, i, step, tuple(_init_carry(sub, dv) for _ in range(n_chain))))
    mask = _tri_mask(sub)
    for c in range(nsub):
        blk = i * nsub + c
        v = values(blk, 1)
        for t in range(n_chain):
            r = t // chains_per_sub
            if r >= c:
                st[t] = _softmax_pv(logits(t, blk, 1), v, st[t], mask if r == c else None)
    return st


def _key_rows(blk, nb):
    return pl.ds(pl.multiple_of(blk * FLASH_SUB, FLASH_SUB), nb * FLASH_SUB)


def _silu(g):
    return g * jax.nn.sigmoid(g)


def _mla_kernel(q_ref, kn_ref, kr_ref, v_ref, gate_ref, cq_ref, sq_ref, ck_ref, sk_ref, o_ref, k_scr, *, bq, scale):
    i = pl.program_id(2)
    half = MLA_ROPE // 2
    sub = FLASH_SUB

    @pl.when(i == 0)
    def _():
        k_scr[:, :LANES] = kn_ref[...]
        k_scr[:, LANES:] = _rope(kr_ref[...].astype(F32), ck_ref[...], sk_ref[...], half, LANES).astype(BF16)

    qs = []
    for r in range(bq // sub):
        rs = slice(r * sub, (r + 1) * sub)
        qn = (q_ref[rs, :LANES].astype(F32) * scale).astype(BF16)
        qr = _rope(q_ref[rs, LANES:].astype(F32), cq_ref[rs, :], sq_ref[rs, :], half, LANES).astype(BF16)
        qs.append(jnp.concatenate([qn, qr], axis=1))

    def logits(t, blk, nb):
        return lax.dot_general(qs[t], k_scr[_key_rows(blk, nb), :], _NT, preferred_element_type=F32)

    def values(blk, nb):
        return v_ref[_key_rows(blk, nb), :]

    st = _causal_sweep(logits, values, i, bq, bq // sub, 1, MLA_V)
    for r, (_, l, acc) in enumerate(st):
        rs = slice(r * sub, (r + 1) * sub)
        o_ref[rs, :] = (acc / l * _silu(gate_ref[rs, :].astype(F32))).astype(o_ref.dtype)


def _mla_attention(q, kv, lat, proj, tabs_q, tabs_k, batch, seq, *, bq):
    t = batch * seq
    nq = seq // bq
    kr_block = MLA_KV_RANK // LANES
    qrow = lambda b, h, i: (i, 0)
    full = lambda b, h, i: (0, 0)
    return pl.pallas_call(
        functools.partial(_mla_kernel, bq=bq, scale=(MLA_NOPE + MLA_ROPE) ** -0.5),
        grid=(batch, MLA_HEADS, nq),
        in_specs=[pl.BlockSpec((bq, 2 * LANES), lambda b, h, i: (b * nq + i, h)),
                  pl.BlockSpec((seq, LANES), lambda b, h, i: (b, h)),
                  pl.BlockSpec((seq, LANES), lambda b, h, i: (b, kr_block)),
                  pl.BlockSpec((seq, LANES), lambda b, h, i: (b, MLA_HEADS + h)),
                  pl.BlockSpec((bq, MLA_V), lambda b, h, i: (b * nq + i, h)),
                  pl.BlockSpec((bq, LANES), qrow), pl.BlockSpec((bq, LANES), qrow),
                  pl.BlockSpec((seq, LANES), full), pl.BlockSpec((seq, LANES), full)],
        out_specs=pl.BlockSpec((bq, MLA_V), lambda b, h, i: (b * nq + i, h)),
        out_shape=jax.ShapeDtypeStruct((t, MLA_WIDTH), BF16),
        scratch_shapes=[pltpu.VMEM((seq, 2 * LANES), BF16)],
        compiler_params=_params("arbitrary", "arbitrary", "arbitrary"),
        name="mla_attention",
    )(q, kv, lat, kv, proj, *tabs_q, *tabs_k)


def _diff_kernel(q_ref, k_ref, v_ref, gate_ref, lam_ref, g_ref, cq_ref, sq_ref, ck_ref, sk_ref, o_ref, k_scr,
                 *, bq, lambda_init):
    i = pl.program_id(2)
    half = DIFF_DIM // ROPE_FRACTION // 2
    sub = FLASH_SUB

    @pl.when(i == 0)
    def _():
        for c in range(2):
            sl = slice(c * LANES, (c + 1) * LANES)
            k_scr[:, sl] = _rope(k_ref[:, sl].astype(F32), ck_ref[...], sk_ref[...], half, LANES).astype(BF16)

    qs = []
    for r in range(bq // sub):
        rs = slice(r * sub, (r + 1) * sub)
        for c in range(2):
            x = q_ref[rs, c * LANES:(c + 1) * LANES].astype(F32)
            qs.append(_rope(x, cq_ref[rs, :], sq_ref[rs, :], half, LANES).astype(BF16))

    def logits(t, blk, nb):
        c = t % 2
        return lax.dot_general(qs[t], k_scr[_key_rows(blk, nb), c * LANES:(c + 1) * LANES], _NT,
                               preferred_element_type=F32)

    def values(blk, nb):
        return v_ref[_key_rows(blk, nb), :]

    st = _causal_sweep(logits, values, i, bq, 2 * (bq // sub), 2, 2 * DIFF_DIM)

    lp = lam_ref[...]
    lam = (jnp.exp(jnp.sum(lp[0:1] * lp[1:2], axis=1, keepdims=True))
           - jnp.exp(jnp.sum(lp[2:3] * lp[3:4], axis=1, keepdims=True)) + lambda_init)
    for r in range(bq // sub):
        rs = slice(r * sub, (r + 1) * sub)
        (_, l0, acc0), (_, l1, acc1) = st[2 * r], st[2 * r + 1]
        o = acc0 / l0 - lam * (acc1 / l1)
        o = o * lax.rsqrt(jnp.mean(o * o, axis=-1, keepdims=True) + NORM_EPS) * g_ref[...]
        o_ref[rs, :] = (o * (1.0 - lambda_init) * _silu(gate_ref[rs, :].astype(F32))).astype(o_ref.dtype)


def _diff_attention(proj, lam_params, subln_g, tabs_q, tabs_k, batch, seq, lambda_init, *, bq):
    t = batch * seq
    nq = seq // bq
    w = 2 * DIFF_DIM
    q0 = MLA_WIDTH // w
    k0, v0, g0 = q0 + DIFF_HEADS, q0 + 2 * DIFF_HEADS, q0 + 3 * DIFF_HEADS
    qrow = lambda b, h, i: (i, 0)
    full = lambda b, h, i: (0, 0)
    return pl.pallas_call(
        functools.partial(_diff_kernel, bq=bq, lambda_init=lambda_init),
        grid=(batch, DIFF_HEADS, nq),
        in_specs=[pl.BlockSpec((bq, w), lambda b, h, i: (b * nq + i, q0 + h)),
                  pl.BlockSpec((seq, w), lambda b, h, i: (b, k0 + h)),
                  pl.BlockSpec((seq, w), lambda b, h, i: (b, v0 + h)),
                  pl.BlockSpec((bq, w), lambda b, h, i: (b * nq + i, g0 + h)),
                  pl.BlockSpec((4, DIFF_DIM), full),
                  pl.BlockSpec((1, w), full),
                  pl.BlockSpec((bq, LANES), qrow), pl.BlockSpec((bq, LANES), qrow),
                  pl.BlockSpec((seq, LANES), full), pl.BlockSpec((seq, LANES), full)],
        out_specs=pl.BlockSpec((bq, w), lambda b, h, i: (b * nq + i, h)),
        out_shape=jax.ShapeDtypeStruct((t, DIFF_WIDTH), BF16),
        scratch_shapes=[pltpu.VMEM((seq, w), BF16)],
        compiler_params=_params("arbitrary", "arbitrary", "arbitrary"),
        name="diff_attention",
    )(proj, proj, proj, proj, lam_params.astype(F32), subln_g.reshape(1, w).astype(F32), *tabs_q, *tabs_k)


def _fox_kernel(q_ref, k_ref, v_ref, gate_ref, cum_ref, cumt_ref, o_ref, *, bq, scale):
    h = pl.program_id(1)
    i = pl.program_id(2)
    sub = FLASH_SUB
    lane = lax.broadcasted_iota(jnp.int32, (sub, LANES), 1)
    qs, cqs = [], []
    for r in range(bq // sub):
        rs = slice(r * sub, (r + 1) * sub)
        qs.append((q_ref[rs, :].astype(F32) * scale).astype(BF16))
        cqs.append(jnp.sum(jnp.where(lane == FOX_F_LANE + h, cum_ref[rs, :], 0.0), axis=1, keepdims=True))

    def logits(t, blk, nb):
        s = lax.dot_general(qs[t], k_ref[_key_rows(blk, nb), :], _NT, preferred_element_type=F32)
        ck = jnp.concatenate([cumt_ref[0, pl.ds(blk + n, 1), :] for n in range(nb)], axis=1)
        return s + (cqs[t] - ck)

    def values(blk, nb):
        return v_ref[_key_rows(blk, nb), :]

    st = _causal_sweep(logits, values, i, bq, bq // sub, 1, FOX_DIM)
    for r, (_, l, acc) in enumerate(st):
        rs = slice(r * sub, (r + 1) * sub)
        o_ref[rs, :] = (acc / l * _silu(gate_ref[rs, :].astype(F32))).astype(o_ref.dtype)


def _fox_attention(proj, cum, cum_t, batch, seq, *, bq):
    t = batch * seq
    nq = seq // bq
    q0 = DSA_WIDTH // LANES
    k0, v0, g0 = q0 + FOX_HEADS, q0 + 2 * FOX_HEADS, q0 + 3 * FOX_HEADS
    return pl.pallas_call(
        functools.partial(_fox_kernel, bq=bq, scale=FOX_DIM ** -0.5),
        grid=(batch, FOX_HEADS, nq),
        in_specs=[pl.BlockSpec((bq, LANES), lambda b, h, i: (b * nq + i, q0 + h)),
                  pl.BlockSpec((seq, LANES), lambda b, h, i: (b, k0 + h)),
                  pl.BlockSpec((seq, LANES), lambda b, h, i: (b, v0 + h)),
                  pl.BlockSpec((bq, LANES), lambda b, h, i: (b * nq + i, g0 + h)),
                  pl.BlockSpec((bq, LANES), lambda b, h, i: (b * nq + i, 0)),
                  pl.BlockSpec((1, seq // FLASH_SUB, FLASH_SUB), lambda b, h, i: (b * FOX_HEADS + h, 0, 0))],
        out_specs=pl.BlockSpec((bq, FOX_DIM), lambda b, h, i: (b * nq + i, h)),
        out_shape=jax.ShapeDtypeStruct((t, FOX_WIDTH), BF16),
        compiler_params=_params("arbitrary", "arbitrary", "arbitrary"),
        name="fox_attention",
    )(proj, proj, proj, proj, cum, cum_t)


def _logf_cumsum_kernel(f_ref, b_ref, o_ref, carry_ref, *, tb):
    @pl.when(pl.program_id(1) == 0)
    def _():
        carry_ref[...] = jnp.zeros_like(carry_ref)

    x = f_ref[...] + b_ref[...]
    logf = jnp.minimum(x, 0.0) - jnp.log1p(jnp.exp(-jnp.abs(x)))
    row = lax.broadcasted_iota(jnp.int32, (tb, tb), 0)
    col = lax.broadcasted_iota(jnp.int32, (tb, tb), 1)
    tri = jnp.where(row >= col, 1.0, 0.0).astype(BF16)
    hi = logf.astype(BF16)
    r1 = logf - hi.astype(F32)
    mid = r1.astype(BF16)
    lo = (r1 - mid.astype(F32)).astype(BF16)
    cum = (jnp.dot(tri, hi, preferred_element_type=F32) + jnp.dot(tri, mid, preferred_element_type=F32)
           + jnp.dot(tri, lo, preferred_element_type=F32)) + carry_ref[...]
    o_ref[...] = cum
    carry_ref[...] = cum[tb - 1:tb, :]


def _logf_cumsum(small, forget_bias, batch, seq, *, tb=512):
    t = batch * seq
    nb = seq // tb
    bias = jnp.zeros((1, LANES), F32).at[0, FOX_F_LANE:FOX_F_LANE + FOX_HEADS].set(forget_bias.astype(F32))
    return pl.pallas_call(
        functools.partial(_logf_cumsum_kernel, tb=tb),
        grid=(batch, nb),
        in_specs=[pl.BlockSpec((tb, LANES), lambda b, i: (b * nb + i, 2)),
                  pl.BlockSpec((1, LANES), lambda b, i: (0, 0))],
        out_specs=pl.BlockSpec((tb, LANES), lambda b, i: (b * nb + i, 0)),
        out_shape=jax.ShapeDtypeStruct((t, LANES), F32),
        scratch_shapes=[pltpu.VMEM((1, LANES), F32)],
        compiler_params=_params("arbitrary", "arbitrary"),
        name="logf_cumsum",
    )(small, bias)


DSA_CK = 512
DSA_BQ = 256
DSA_GROUP = 2
DSA_MAX_ITERS = 640
DSA_PROBES_PER_CHECK = 4


def _row_total(x):
    return jnp.broadcast_to(jnp.sum(x, axis=1, keepdims=True), x.shape)


def _tile_lanes(x, n):
    return jnp.concatenate([x] * n, axis=1)


def _dsa_kernel(qc_ref, qi_ref, wq_ref, kv_ref, gate_ref, cdq_ref, sdq_ref, cdk_ref, sdk_ref, ciq_ref, siq_ref,
                cik_ref, sik_ref, o_ref, kc_scr, vc_scr, ki_scr, idx_scr, q_scr, qi_scr, w_scr, *, seq, topk):
    i = pl.program_id(1)
    bq, ck = DSA_BQ, DSA_CK
    dhalf = DSA_DIM // ROPE_FRACTION // 2
    ihalf = IDX_DIM // ROPE_FRACTION // 2
    rows = DSA_HEADS * bq
    lane = lax.broadcasted_iota(jnp.int32, (bq, LANES), 1)

    @pl.when(i == 0)
    def _():
        tb = 512

        def prep(r, _):
            off = pl.multiple_of(r * tb, tb)
            sl = pl.ds(off, tb)
            kc_scr[sl, :] = _rope(kv_ref[sl, 0:LANES], cdk_ref[sl, :], sdk_ref[sl, :], dhalf, LANES).astype(BF16)
            vc_scr[sl, :LANES] = kv_ref[sl, LANES:2 * LANES].astype(BF16)
            vc_scr[sl, LANES:] = jnp.ones((tb, LANES), BF16)
            lane_t = lax.broadcasted_iota(jnp.int32, (tb, LANES), 1)
            ki = jnp.where(lane_t < IDX_DIM, kv_ref[sl, 2 * LANES:3 * LANES], 0.0)
            ki = _rope(ki, cik_ref[sl, :], sik_ref[sl, :], ihalf, IDX_DIM)
            ki_scr[sl, :] = (ki + pltpu.roll(ki, IDX_DIM, axis=1)).astype(BF16)
            return 0

        lax.fori_loop(0, seq // tb, prep, 0)

    for h in range(DSA_HEADS):
        xh = qc_ref[:, h * LANES:(h + 1) * LANES].astype(F32)
        q_scr[h * bq:(h + 1) * bq, :] = _rope(xh, cdq_ref[...], sdq_ref[...], dhalf, LANES).astype(BF16)
    for p in range(IDX_HEADS // 2):
        xp = _rope(qi_ref[:, p * LANES:(p + 1) * LANES].astype(F32), ciq_ref[...], siq_ref[...], ihalf, IDX_DIM)
        qi_scr[(2 * p) * bq:(2 * p + 1) * bq, :] = jnp.where(lane < IDX_DIM, xp, 0.0).astype(BF16)
        qi_scr[(2 * p + 1) * bq:(2 * p + 2) * bq, :] = jnp.where(lane >= IDX_DIM, xp, 0.0).astype(BF16)
    w = wq_ref[:, 2 * LANES:3 * LANES] * ((IDX_HEADS ** -0.5) * (IDX_DIM ** -0.5))
    for h in range(IDX_HEADS):
        w_scr[h * bq:(h + 1) * bq, :] = _row_total(jnp.where(lane == IDX_W_LANE + h, w, 0.0))

    nch = i // (ck // bq) + 1
    qpos = i * bq + lax.broadcasted_iota(jnp.int32, (bq, LANES), 0)
    inf = jnp.full((bq, LANES), jnp.inf, F32)

    def idx_step(j, carry):
        mn, mx = carry
        off = pl.multiple_of(j * ck, ck)
        r = lax.dot_general(qi_scr[...], ki_scr[pl.ds(off, ck), :], _NT, preferred_element_type=F32)
        halves = []
        for c in range(ck // LANES):
            sc = None
            for h in range(IDX_HEADS):
                term = jnp.maximum(r[h * bq:(h + 1) * bq, c * LANES:(c + 1) * LANES], 0.0) * w_scr[h * bq:(h + 1) * bq, :]
                sc = term if sc is None else sc + term
            valid = (off + c * LANES + lane) <= qpos
            halves.append(jnp.where(valid, sc, -inf))
            mn = jnp.minimum(mn, jnp.where(valid, sc, inf))
            mx = jnp.maximum(mx, jnp.where(valid, sc, -inf))
        idx_scr[j] = jnp.concatenate(halves, axis=1)
        return mn, mx

    mn, mx = lax.fori_loop(0, nch, idx_step, (inf, -inf))
    mn = jnp.broadcast_to(jnp.min(mn, axis=1, keepdims=True), mn.shape)
    mx = jnp.broadcast_to(jnp.max(mx, axis=1, keepdims=True), mx.shape)

    def count_ge(t):
        nslab = bq // LANES

        def body(j, cs):
            out = []
            for u in range(nslab):
                rs = slice(u * LANES, (u + 1) * LANES)
                x = idx_scr[j, rs, :]
                c = cs[u]
                for half in range(ck // LANES):
                    c = c + jnp.where(x[:, half * LANES:(half + 1) * LANES] >= t[rs], 1.0, 0.0)
                out.append(c)
            return tuple(out)

        cs = lax.fori_loop(0, nch, body, tuple(jnp.zeros((LANES, LANES), F32) for _ in range(nslab)))
        return _row_total(jnp.concatenate(cs, axis=0))

    kf = float(topk)
    nvalid = (qpos + 1).astype(F32)
    c_mx = count_ge(mx)
    few = nvalid <= kf
    top_tie = (~few) & (c_mx >= kf)
    lo = jnp.where(top_tie, mx, mn)
    c_lo = jnp.where(top_tie, c_mx, nvalid)
    hi = jnp.where(top_tie, inf, mx)
    c_hi = jnp.where(top_tie, 0.0, c_mx)
    done = jnp.where(few | top_tie | (c_lo == kf), 1.0, 0.0)

    def search_cond(st):
        it, _, _, _, _, done = st
        return jnp.logical_and(it < DSA_MAX_ITERS, jnp.min(done) < 0.5)

    def probe(st):
        lo, hi, c_lo, c_hi, done = st
        mid = 0.5 * lo + 0.5 * hi
        adjacent = (mid <= lo) | (mid >= hi)
        c = count_ge(mid)
        ge = c >= kf
        upd = (done < 0.5) & (~adjacent)
        lo = jnp.where(upd & ge, mid, lo)
        c_lo = jnp.where(upd & ge, c, c_lo)
        hi = jnp.where(upd & (~ge), mid, hi)
        c_hi = jnp.where(upd & (~ge), c, c_hi)
        done = jnp.where(adjacent | (c_lo == kf), 1.0, done)
        return lo, hi, c_lo, c_hi, done

    def search_body(st):
        it, rest = st[0], st[1:]
        for _ in range(DSA_PROBES_PER_CHECK):
            rest = probe(rest)
        return (it + DSA_PROBES_PER_CHECK,) + tuple(rest)

    _, lo, hi, c_lo, c_hi, _ = lax.while_loop(search_cond, search_body, (jnp.int32(0), lo, hi, c_lo, c_hi, done))

    nl = ck // LANES
    lo_w = _tile_lanes(lo, nl)

    def write_plain():
        def body(j, _):
            idx_scr[j] = jnp.where(idx_scr[j] >= lo_w, 0.0, MASKED)
            return 0
        lax.fori_loop(0, nch, body, 0)

    def write_ties():
        hi_w = _tile_lanes(hi, nl)
        need = _tile_lanes(kf - c_hi, nl)
        urow = lax.broadcasted_iota(jnp.int32, (ck, ck), 0)
        ucol = lax.broadcasted_iota(jnp.int32, (ck, ck), 1)
        upper = jnp.where(urow < ucol, 1.0, 0.0).astype(BF16)

        def body(j, before):
            x = idx_scr[j]
            above = x >= hi_w
            eq = jnp.where((x >= lo_w) & (~above), 1.0, 0.0)
            prefix = jnp.dot(eq.astype(BF16), upper, preferred_element_type=F32) + _tile_lanes(before, nl)
            sel = above | ((eq > 0.5) & (prefix < need))
            idx_scr[j] = jnp.where(sel, 0.0, MASKED)
            tot = eq[:, :LANES]
            for c in range(1, nl):
                tot = tot + eq[:, c * LANES:(c + 1) * LANES]
            return before + _row_total(tot)

        lax.fori_loop(0, nch, body, jnp.zeros((bq, LANES), F32))

    lax.cond(jnp.max(c_lo) > kf, write_ties, write_plain)

    grp = DSA_GROUP * bq
    ngrp = rows // grp

    def att_step(j, carry):
        off = pl.multiple_of(j * ck, ck)
        k = kc_scr[pl.ds(off, ck), :]
        v = vc_scr[pl.ds(off, ck), :]
        bias = jnp.concatenate([idx_scr[j]] * DSA_GROUP, axis=0)
        out = []
        for g in range(ngrp):
            m, acc = carry[g]
            s = lax.dot_general(q_scr[g * grp:(g + 1) * grp, :], k, _NT, preferred_element_type=F32)
            parts = [s[:, c * LANES:(c + 1) * LANES] + bias[:, c * LANES:(c + 1) * LANES] for c in range(nl)]
            mc = parts[0]
            for c in range(1, nl):
                mc = jnp.maximum(mc, parts[c])
            m_new = jnp.maximum(m, jnp.max(mc, axis=1, keepdims=True))
            alpha = jnp.exp(m - m_new)
            p = jnp.concatenate([jnp.exp(x - m_new) for x in parts], axis=1).astype(BF16)
            acc = _tile_lanes(alpha, 2) * acc + jnp.dot(p, v, preferred_element_type=F32)
            out.append((m_new, acc))
        return tuple(out)

    init = tuple((jnp.full((grp, LANES), MASKED, F32), jnp.zeros((grp, 2 * LANES), F32)) for _ in range(ngrp))
    res = lax.fori_loop(0, nch, att_step, init)
    for g in range(ngrp):
        _, acc = res[g]
        o = acc[:, :LANES] / acc[:, LANES:]
        for t in range(DSA_GROUP):
            hl = slice((g * DSA_GROUP + t) * LANES, (g * DSA_GROUP + t + 1) * LANES)
            o_ref[:, hl] = (o[t * bq:(t + 1) * bq] * _silu(gate_ref[:, hl].astype(F32))).astype(o_ref.dtype)


def _dsa_attention(proj, small, dsa_tabs_q, dsa_tabs_k, idx_tabs, batch, seq):
    t = batch * seq
    bq = DSA_BQ
    nq = seq // bq
    topk = min(TOPK_MAX, seq // 4)
    rows = DSA_HEADS * bq
    idx_w = IDX_HEADS * IDX_DIM
    c_dsa_q = 2 * DSA_WIDTH + 3 * FOX_WIDTH
    qrow = lambda b, i: (i, 0)
    full = lambda b, i: (0, 0)
    return pl.pallas_call(
        functools.partial(_dsa_kernel, seq=seq, topk=topk),
        grid=(batch, nq),
        in_specs=[pl.BlockSpec((bq, DSA_WIDTH), lambda b, i: (b * nq + i, c_dsa_q // DSA_WIDTH)),
                  pl.BlockSpec((bq, idx_w), lambda b, i: (b * nq + i, (c_dsa_q + DSA_WIDTH) // idx_w)),
                  pl.BlockSpec((bq, 3 * LANES), lambda b, i: (b * nq + i, 0)),
                  pl.BlockSpec((seq, 3 * LANES), lambda b, i: (b, 0)),
                  pl.BlockSpec((bq, DSA_WIDTH), lambda b, i: (b * nq + i, 0)),
                  pl.BlockSpec((bq, LANES), qrow), pl.BlockSpec((bq, LANES), qrow),
                  pl.BlockSpec((seq, LANES), full), pl.BlockSpec((seq, LANES), full),
                  pl.BlockSpec((bq, LANES), qrow), pl.BlockSpec((bq, LANES), qrow),
                  pl.BlockSpec((seq, LANES), full), pl.BlockSpec((seq, LANES), full)],
        out_specs=pl.BlockSpec((bq, DSA_WIDTH), lambda b, i: (b * nq + i, 0)),
        out_shape=jax.ShapeDtypeStruct((t, DSA_WIDTH), BF16),
        scratch_shapes=[pltpu.VMEM((seq, LANES), BF16), pltpu.VMEM((seq, 2 * LANES), BF16),
                        pltpu.VMEM((seq, LANES), BF16), pltpu.VMEM((seq // DSA_CK, bq, DSA_CK), F32),
                        pltpu.VMEM((rows, LANES), BF16), pltpu.VMEM((rows, LANES), BF16),
                        pltpu.VMEM((rows, LANES), F32)],
        compiler_params=_params("arbitrary", "arbitrary"),
        name="dsa_attention",
    )(proj, proj, small, small, proj, *dsa_tabs_q, *dsa_tabs_k, *idx_tabs, *idx_tabs)


def _even_layer(x2, batch, seq, norm_g, w_in, q_norm_g, w_uq, kv_norm_g, w_ukv, diff_lambda, subln_g, w_out,
                lambda_init, bq):
    wt = w_in.T
    n_lat = MLA_Q_RANK + MLA_KV_RANK + MLA_ROPE
    n_big = 2 * MLA_WIDTH + 3 * DIFF_WIDTH
    n_kv_tile = MLA_KV_RANK + LANES
    uq = w_uq.astype(BF16).reshape(MLA_Q_RANK, MLA_HEADS, MLA_NOPE + MLA_ROPE)
    uq = jnp.concatenate([uq, jnp.zeros((MLA_Q_RANK, MLA_HEADS, 2 * LANES - MLA_NOPE - MLA_ROPE), BF16)], axis=2)
    uq = uq.reshape(MLA_Q_RANK, MLA_HEADS * 2 * LANES)
    ukv = w_ukv.astype(BF16).reshape(MLA_KV_RANK, MLA_HEADS, MLA_NOPE + MLA_V)
    ukv = jnp.concatenate([ukv[:, :, :MLA_NOPE].reshape(MLA_KV_RANK, -1), ukv[:, :, MLA_NOPE:].reshape(MLA_KV_RANK, -1)],
                          axis=1)

    h = _rmsnorm(x2, norm_g, BF16)
    proj = _matmul_nt(h, wt, [(n_lat, n_big)], BF16, tm=1024, tn=1024, name="even_inproj")
    lat_q = _matmul_nt(h, wt, [(0, MLA_Q_RANK)], BF16, tm=1024, tn=MLA_Q_RANK, name="even_latent_q")
    lat_kv = _matmul_nt(h, wt, [(MLA_Q_RANK, n_kv_tile)], BF16, tm=1024, tn=n_kv_tile, name="even_latent_kv")
    q = _norm_matmul(lat_q, 0, q_norm_g, uq, tm=1024, tn=1024, name="mla_q_up")
    kv = _norm_matmul(lat_kv, 0, kv_norm_g, ukv, tm=1024, tn=1024, name="mla_kv_up")

    mla_scale = (MLA_NOPE + MLA_ROPE) ** -0.5
    o_a = _mla_attention(q, kv, lat_kv, proj, _rope_tables(seq, MLA_ROPE, LANES, mla_scale),
                         _rope_tables(seq, MLA_ROPE, LANES), batch, seq, bq=bq)
    rot = DIFF_DIM // ROPE_FRACTION
    o_b = _diff_attention(proj, diff_lambda, subln_g, _rope_tables(seq, rot, LANES, DIFF_DIM ** -0.5),
                          _rope_tables(seq, rot, LANES), batch, seq, lambda_init, bq=min(bq, DIFF_BQ))
    return _outproj(o_a, o_b, w_out.astype(BF16), x2, tm=1024, tn=1024)


def _odd_layer(x2, batch, seq, norm_g, w_in, forget_bias, w_out, bq):
    wt = w_in.T
    c_dsa_k = DSA_WIDTH
    c_idx_q = c_dsa_k + 2 * DSA_DIM
    c_idx_k = c_idx_q + IDX_HEADS * IDX_DIM
    c_gate_c = c_idx_k + IDX_DIM + IDX_HEADS
    c_fox_f = c_gate_c + DSA_WIDTH + 3 * FOX_WIDTH
    c_gate_d = c_fox_f + FOX_HEADS
    big_rows = [(c_gate_c, DSA_WIDTH + 3 * FOX_WIDTH), (c_gate_d, FOX_WIDTH), (0, DSA_WIDTH),
                (c_idx_q, IDX_HEADS * IDX_DIM)]
    n_small = 2 * DSA_DIM + IDX_DIM + IDX_HEADS + FOX_HEADS
    w_small = jnp.concatenate([wt[c_dsa_k:c_dsa_k + 2 * DSA_DIM], wt[c_idx_k:c_idx_k + IDX_DIM + IDX_HEADS],
                               wt[c_fox_f:c_fox_f + FOX_HEADS],
                               jnp.zeros((3 * LANES - n_small, wt.shape[1]), wt.dtype)], axis=0)

    h = _rmsnorm(x2, norm_g, BF16)
    proj = _matmul_nt(h, wt, big_rows, BF16, tm=1024, tn=1024, name="odd_inproj")
    small = _matmul_nt(h, w_small, [(0, 3 * LANES)], F32, tm=1024, tn=3 * LANES, name="odd_small")

    rot = DSA_DIM // ROPE_FRACTION
    o_c = _dsa_attention(proj, small, _rope_tables(seq, rot, LANES, DSA_DIM ** -0.5), _rope_tables(seq, rot, LANES),
                         _rope_tables(seq, IDX_DIM // ROPE_FRACTION, IDX_DIM), batch, seq)
    cum = _logf_cumsum(small, forget_bias, batch, seq)
    cum_t = cum[:, FOX_F_LANE:FOX_F_LANE + FOX_HEADS].reshape(batch, seq, FOX_HEADS).transpose(0, 2, 1)
    cum_t = cum_t.reshape(batch * FOX_HEADS, seq // FLASH_SUB, FLASH_SUB)
    o_d = _fox_attention(proj, cum, cum_t, batch, seq, bq=bq)
    return _outproj(o_c, o_d, w_out.astype(BF16), x2, tm=1024, tn=1024)


def kernel(x, even_norm, even_w_in, mla_q_norm, mla_w_uq, mla_kv_norm, mla_w_ukv, diff_lambda, diff_subln, even_w_out, odd_norm, odd_w_in, fox_forget_bias, odd_w_out, final_norm):
    batch, seq, d = x.shape
    bq = min(FLASH_BQ, seq)
    h = x.reshape(batch * seq, d)
    depth = even_norm.shape[0] + odd_norm.shape[0]
    for layer in range(depth):
        i = layer // 2
        if layer % 2 == 0:
            lambda_init = 0.8 - 0.6 * math.exp(-0.3 * layer)
            h = _even_layer(h, batch, seq, even_norm[i], even_w_in[i], mla_q_norm[i], mla_w_uq[i], mla_kv_norm[i],
                            mla_w_ukv[i], diff_lambda[i], diff_subln[i], even_w_out[i], lambda_init, bq)
        else:
            h = _odd_layer(h, batch, seq, odd_norm[i], odd_w_in[i], fox_forget_bias[i], odd_w_out[i], bq)
    return _rmsnorm(h, final_norm, x.dtype).reshape(batch, seq, d)
```

```python
import functools
import math

import jax
import jax.numpy as jnp
from jax import lax
from jax.experimental import pallas as pl
from jax.experimental.pallas import tpu as pltpu

F32 = jnp.float32
BF16 = jnp.bfloat16

ROPE_THETA = 500000.0
NORM_EPS = 1e-6
ROPE_FRACTION = 4

MLA_HEADS, MLA_NOPE, MLA_ROPE, MLA_V = 16, 128, 64, 128
MLA_Q_RANK, MLA_KV_RANK = 1024, 512
DIFF_HEADS, DIFF_DIM = 8, 128
DSA_HEADS, DSA_DIM = 16, 128
IDX_HEADS, IDX_DIM = 16, 64
TOPK_MAX = 256
FOX_HEADS, FOX_DIM = 16, 128

MLA_WIDTH = MLA_HEADS * MLA_V
DIFF_WIDTH = DIFF_HEADS * 2 * DIFF_DIM
DSA_WIDTH = DSA_HEADS * DSA_DIM
FOX_WIDTH = FOX_HEADS * FOX_DIM

LANES = 128
VMEM_LIMIT_BYTES = 56 * 1024 * 1024
MATMUL_VMEM_LIMIT_BYTES = 58 * 1024 * 1024
MASKED = -1e30

IDX_W_LANE = IDX_DIM
FOX_F_LANE = IDX_DIM + IDX_HEADS

_NT = (((1,), (1,)), ((), ()))


def _params(*sem):
    return pltpu.CompilerParams(dimension_semantics=sem, vmem_limit_bytes=VMEM_LIMIT_BYTES)


def _rmsnorm_kernel(x_ref, g_ref, o_ref):
    x = x_ref[...].astype(F32)
    y = x * lax.rsqrt(jnp.mean(x * x, axis=-1, keepdims=True) + NORM_EPS)
    o_ref[...] = (y * g_ref[...]).astype(o_ref.dtype)


def _rmsnorm(x, g, out_dtype, tm=256):
    t, d = x.shape
    return pl.pallas_call(
        _rmsnorm_kernel,
        grid=(t // tm,),
        in_specs=[pl.BlockSpec((tm, d), lambda i: (i, 0)), pl.BlockSpec((1, d), lambda i: (0, 0))],
        out_specs=pl.BlockSpec((tm, d), lambda i: (i, 0)),
        out_shape=jax.ShapeDtypeStruct((t, d), out_dtype),
        compiler_params=_params("arbitrary"),
        name="rmsnorm",
    )(x, g.reshape(1, d).astype(F32))


def _mm_nt_kernel(a_ref, bt_ref, o_ref):
    tn = bt_ref.shape[0]
    w = tn // 2 if tn % (2 * LANES) == 0 else tn
    for p in range(tn // w):
        bt = bt_ref[p * w:(p + 1) * w, :].astype(BF16)
        o_ref[:, p * w:(p + 1) * w] = lax.dot_general(a_ref[...], bt, _NT,
                                                      preferred_element_type=F32).astype(o_ref.dtype)


SUBLANES = 8


def _matmul_nt(a, bt, row_ranges, out_dtype, *, tm, tn, name):
    m, kdim = a.shape
    starts = []
    for first, n_rows in row_ranges:
        assert first % SUBLANES == 0 and n_rows % tn == 0, (first, n_rows)
        starts += [first + t * tn for t in range(n_rows // tn)]
    n = len(starts) * tn

    def bt_map(j, i):
        q = starts[-1] // SUBLANES
        for t in range(len(starts) - 2, -1, -1):
            q = jnp.where(j <= t, starts[t] // SUBLANES, q)
        return q * SUBLANES, 0

    return pl.pallas_call(
        _mm_nt_kernel,
        grid=(len(starts), m // tm),
        in_specs=[pl.BlockSpec((tm, kdim), lambda j, i: (i, 0)),
                  pl.BlockSpec((pl.Element(tn), pl.Element(kdim)), bt_map)],
        out_specs=pl.BlockSpec((tm, tn), lambda j, i: (i, j)),
        out_shape=jax.ShapeDtypeStruct((m, n), out_dtype),
        compiler_params=pltpu.CompilerParams(dimension_semantics=("arbitrary", "arbitrary"),
                                             vmem_limit_bytes=MATMUL_VMEM_LIMIT_BYTES),
        name=name,
    )(a, bt)


def _norm_mm_kernel(a_ref, g_ref, b_ref, o_ref):
    x = a_ref[...].astype(F32)
    y = x * lax.rsqrt(jnp.mean(x * x, axis=-1, keepdims=True) + NORM_EPS) * g_ref[...]
    o_ref[...] = jnp.dot(y.astype(BF16), b_ref[...], preferred_element_type=F32).astype(o_ref.dtype)


def _norm_matmul(a, a_col_block, g, b, *, tm, tn, name):
    m = a.shape[0]
    kdim, n = b.shape
    return pl.pallas_call(
        _norm_mm_kernel,
        grid=(m // tm, n // tn),
        in_specs=[pl.BlockSpec((tm, kdim), lambda i, j: (i, a_col_block)),
                  pl.BlockSpec((1, kdim), lambda i, j: (0, 0)),
                  pl.BlockSpec((kdim, tn), lambda i, j: (0, j))],
        out_specs=pl.BlockSpec((tm, tn), lambda i, j: (i, j)),
        out_shape=jax.ShapeDtypeStruct((m, n), BF16),
        compiler_params=_params("arbitrary", "arbitrary"),
        name=name,
    )(a, g.reshape(1, kdim).astype(F32), b)


def _outproj_kernel(oa_ref, ob_ref, w_ref, res_ref, h_ref):
    ka = oa_ref.shape[1]
    acc = jnp.dot(oa_ref[...], w_ref[:ka, :].astype(BF16), preferred_element_type=F32)
    acc = acc + jnp.dot(ob_ref[...], w_ref[ka:, :].astype(BF16), preferred_element_type=F32)
    h_ref[...] = acc + res_ref[...]


def _outproj(oa, ob, w, res, *, tm, tn):
    m, ka = oa.shape
    kb = ob.shape[1]
    n = w.shape[1]
    return pl.pallas_call(
        _outproj_kernel,
        grid=(m // tm, n // tn),
        in_specs=[pl.BlockSpec((tm, ka), lambda i, j: (i, 0)),
                  pl.BlockSpec((tm, kb), lambda i, j: (i, 0)),
                  pl.BlockSpec((ka + kb, tn), lambda i, j: (0, j)),
                  pl.BlockSpec((tm, tn), lambda i, j: (i, j))],
        out_specs=pl.BlockSpec((tm, tn), lambda i, j: (i, j)),
        out_shape=jax.ShapeDtypeStruct((m, n), F32),
        compiler_params=_params("arbitrary", "arbitrary"),
        name="outproj",
    )(oa, ob, w, res)


def _rope_tables(seq, rot_dim, period, scale=1.0):
    inv = ROPE_THETA ** (-jnp.arange(0, rot_dim, 2, dtype=F32) / rot_dim)
    ang = jnp.arange(seq, dtype=F32)[:, None] * inv[None, :]
    cos, sin = jnp.cos(ang), jnp.sin(ang)
    ones = jnp.ones((seq, period - rot_dim), F32)
    c = jnp.concatenate([cos, cos, ones], axis=1) * scale
    s = jnp.concatenate([-sin, sin, 0.0 * ones], axis=1) * scale
    reps = LANES // period
    return jnp.tile(c, (1, reps)), jnp.tile(s, (1, reps))


def _rope(x, c, s, half, period):
    lane = lax.broadcasted_iota(jnp.int32, x.shape, 1)
    first = (lane & (period - 1)) < half
    partner = jnp.where(first, pltpu.roll(x, LANES - half, axis=1), pltpu.roll(x, half, axis=1))
    return x * c + partner * s


def _softmax_pv(s, v, carry, mask=None):
    m, l, acc = carry
    if mask is not None:
        s = jnp.where(mask, s, MASKED)
    m_new = jnp.maximum(m, jnp.max(s, axis=1, keepdims=True))
    alpha = jnp.exp(m - m_new)
    p = jnp.exp(s - m_new)
    l = alpha * l + jnp.sum(p, axis=1, keepdims=True)
    acc = alpha * acc + jnp.dot(p.astype(BF16), v, preferred_element_type=F32)
    return m_new, l, acc


def _init_carry(rows, dv):
    return (jnp.full((rows, 1), MASKED, F32), jnp.zeros((rows, 1), F32), jnp.zeros((rows, dv), F32))


FLASH_SUB = 512
FLASH_BQ = 2048
DIFF_BQ = 1024


def _tri_mask(n):
    row = lax.broadcasted_iota(jnp.int32, (n, n), 0)
    col = lax.broadcasted_iota(jnp.int32, (n, n), 1)
    return row >= col


def _causal_sweep(logits, values, i, bq, n_chain, chains_per_sub, dv):
    sub = FLASH_SUB
    nsub = bq // sub

    def step(j, st):
        v = values(j * nsub, nsub)
        return tuple(_softmax_pv(logits(t, j * nsub, nsub), v, st[t]) for t in range(n_chain))

    st = list(lax.fori_loop(0, i, step, tuple(_init_carry(sub, dv) for _ in range(n_chain))))
    mask = _tri_mask(sub)
    for c in range(nsub):
        blk = i * nsub + c
        v = values(blk, 1)
        for t in range(n_chain):
            r = t // chains_per_sub
            if r >= c:
                st[t] = _softmax_pv(logits(t, blk, 1), v, st[t], mask if r == c else None)
    return st


def _key_rows(blk, nb):
    return pl.ds(pl.multiple_of(blk * FLASH_SUB, FLASH_SUB), nb * FLASH_SUB)


def _silu(g):
    return g * jax.nn.sigmoid(g)


def _mla_kernel(q_ref, kn_ref, kr_ref, v_ref, gate_ref, cq_ref, sq_ref, ck_ref, sk_ref, o_ref, k_scr, *, bq, scale):
    i = pl.program_id(2)
    half = MLA_ROPE // 2
    sub = FLASH_SUB

    @pl.when(i == 0)
    def _():
        k_scr[:, :LANES] = kn_ref[...]
        k_scr[:, LANES:] = _rope(kr_ref[...].astype(F32), ck_ref[...], sk_ref[...], half, LANES).astype(BF16)

    qs = []
    for r in range(bq // sub):
        rs = slice(r * sub, (r + 1) * sub)
        qn = (q_ref[rs, :LANES].astype(F32) * scale).astype(BF16)
        qr = _rope(q_ref[rs, LANES:].astype(F32), cq_ref[rs, :], sq_ref[rs, :], half, LANES).astype(BF16)
        qs.append(jnp.concatenate([qn, qr], axis=1))

    def logits(t, blk, nb):
        return lax.dot_general(qs[t], k_scr[_key_rows(blk, nb), :], _NT, preferred_element_type=F32)

    def values(blk, nb):
        return v_ref[_key_rows(blk, nb), :]

    st = _causal_sweep(logits, values, i, bq, bq // sub, 1, MLA_V)
    for r, (_, l, acc) in enumerate(st):
        rs = slice(r * sub, (r + 1) * sub)
        o_ref[rs, :] = (acc / l * _silu(gate_ref[rs, :].astype(F32))).astype(o_ref.dtype)


def _mla_attention(q, kv, lat, proj, tabs_q, tabs_k, batch, seq, *, bq):
    t = batch * seq
    nq = seq // bq
    kr_block = MLA_KV_RANK // LANES
    qrow = lambda b, h, i: (i, 0)
    full = lambda b, h, i: (0, 0)
    return pl.pallas_call(
        functools.partial(_mla_kernel, bq=bq, scale=(MLA_NOPE + MLA_ROPE) ** -0.5),
        grid=(batch, MLA_HEADS, nq),
        in_specs=[pl.BlockSpec((bq, 2 * LANES), lambda b, h, i: (b * nq + i, h)),
                  pl.BlockSpec((seq, LANES), lambda b, h, i: (b, h)),
                  pl.BlockSpec((seq, LANES), lambda b, h, i: (b, kr_block)),
                  pl.BlockSpec((seq, LANES), lambda b, h, i: (b, MLA_HEADS + h)),
                  pl.BlockSpec((bq, MLA_V), lambda b, h, i: (b * nq + i, h)),
                  pl.BlockSpec((bq, LANES), qrow), pl.BlockSpec((bq, LANES), qrow),
                  pl.BlockSpec((seq, LANES), full), pl.BlockSpec((seq, LANES), full)],
        out_specs=pl.BlockSpec((bq, MLA_V), lambda b, h, i: (b * nq + i, h)),
        out_shape=jax.ShapeDtypeStruct((t, MLA_WIDTH), BF16),
        scratch_shapes=[pltpu.VMEM((seq, 2 * LANES), BF16)],
        compiler_params=_params("arbitrary", "arbitrary", "arbitrary"),
        name="mla_attention",
    )(q, kv, lat, kv, proj, *tabs_q, *tabs_k)


def _diff_kernel(q_ref, k_ref, v_ref, gate_ref, lam_ref, g_ref, cq_ref, sq_ref, ck_ref, sk_ref, o_ref, k_scr,
                 *, bq, lambda_init):
    i = pl.program_id(2)
    half = DIFF_DIM // ROPE_FRACTION // 2
    sub = FLASH_SUB

    @pl.when(i == 0)
    def _():
        for c in range(2):
            sl = slice(c * LANES, (c + 1) * LANES)
            k_scr[:, sl] = _rope(k_ref[:, sl].astype(F32), ck_ref[...], sk_ref[...], half, LANES).astype(BF16)

    qs = []
    for r in range(bq // sub):
        rs = slice(r * sub, (r + 1) * sub)
        for c in range(2):
            x = q_ref[rs, c * LANES:(c + 1) * LANES].astype(F32)
            qs.append(_rope(x, cq_ref[rs, :], sq_ref[rs, :], half, LANES).astype(BF16))

    def logits(t, blk, nb):
        c = t % 2
        return lax.dot_general(qs[t], k_scr[_key_rows(blk, nb), c * LANES:(c + 1) * LANES], _NT,
                               preferred_element_type=F32)

    def values(blk, nb):
        return v_ref[_key_rows(blk, nb), :]

    st = _causal_sweep(logits, values, i, bq, 2 * (bq // sub), 2, 2 * DIFF_DIM)

    lp = lam_ref[...]
    lam = (jnp.exp(jnp.sum(lp[0:1] * lp[1:2], axis=1, keepdims=True))
           - jnp.exp(jnp.sum(lp[2:3] * lp[3:4], axis=1, keepdims=True)) + lambda_init)
    for r in range(bq // sub):
        rs = slice(r * sub, (r + 1) * sub)
        (_, l0, acc0), (_, l1, acc1) = st[2 * r], st[2 * r + 1]
        o = acc0 / l0 - lam * (acc1 / l1)
        o = o * lax.rsqrt(jnp.mean(o * o, axis=-1, keepdims=True) + NORM_EPS) * g_ref[...]
        o_ref[rs, :] = (o * (1.0 - lambda_init) * _silu(gate_ref[rs, :].astype(F32))).astype(o_ref.dtype)


def _diff_attention(proj, lam_params, subln_g, tabs_q, tabs_k, batch, seq, lambda_init, *, bq):
    t = batch * seq
    nq = seq // bq
    w = 2 * DIFF_DIM
    q0 = MLA_WIDTH // w
    k0, v0, g0 = q0 + DIFF_HEADS, q0 + 2 * DIFF_HEADS, q0 + 3 * DIFF_HEADS
    qrow = lambda b, h, i: (i, 0)
    full = lambda b, h, i: (0, 0)
    return pl.pallas_call(
        functools.partial(_diff_kernel, bq=bq, lambda_init=lambda_init),
        grid=(batch, DIFF_HEADS, nq),
        in_specs=[pl.BlockSpec((bq, w), lambda b, h, i: (b * nq + i, q0 + h)),
                  pl.BlockSpec((seq, w), lambda b, h, i: (b, k0 + h)),
                  pl.BlockSpec((seq, w), lambda b, h, i: (b, v0 + h)),
                  pl.BlockSpec((bq, w), lambda b, h, i: (b * nq + i, g0 + h)),
                  pl.BlockSpec((4, DIFF_DIM), full),
                  pl.BlockSpec((1, w), full),
                  pl.BlockSpec((bq, LANES), qrow), pl.BlockSpec((bq, LANES), qrow),
                  pl.BlockSpec((seq, LANES), full), pl.BlockSpec((seq, LANES), full)],
        out_specs=pl.BlockSpec((bq, w), lambda b, h, i: (b * nq + i, h)),
        out_shape=jax.ShapeDtypeStruct((t, DIFF_WIDTH), BF16),
        scratch_shapes=[pltpu.VMEM((seq, w), BF16)],
        compiler_params=_params("arbitrary", "arbitrary", "arbitrary"),
        name="diff_attention",
    )(proj, proj, proj, proj, lam_params.astype(F32), subln_g.reshape(1, w).astype(F32), *tabs_q, *tabs_k)


def _fox_kernel(q_ref, k_ref, v_ref, gate_ref, cum_ref, cumt_ref, o_ref, *, bq, scale):
    h = pl.program_id(1)
    i = pl.program_id(2)
    sub = FLASH_SUB
    lane = lax.broadcasted_iota(jnp.int32, (sub, LANES), 1)
    qs, cqs = [], []
    for r in range(bq // sub):
        rs = slice(r * sub, (r + 1) * sub)
        qs.append((q_ref[rs, :].astype(F32) * scale).astype(BF16))
        cqs.append(jnp.sum(jnp.where(lane == FOX_F_LANE + h, cum_ref[rs, :], 0.0), axis=1, keepdims=True))

    def logits(t, blk, nb):
        s = lax.dot_general(qs[t], k_ref[_key_rows(blk, nb), :], _NT, preferred_element_type=F32)
        ck = jnp.concatenate([cumt_ref[0, pl.ds(blk + n, 1), :] for n in range(nb)], axis=1)
        return s + (cqs[t] - ck)

    def values(blk, nb):
        return v_ref[_key_rows(blk, nb), :]

    st = _causal_sweep(logits, values, i, bq, bq // sub, 1, FOX_DIM)
    for r, (_, l, acc) in enumerate(st):
        rs = slice(r * sub, (r + 1) * sub)
        o_ref[rs, :] = (acc / l * _silu(gate_ref[rs, :].astype(F32))).astype(o_ref.dtype)


def _fox_attention(proj, cum, cum_t, batch, seq, *, bq):
    t = batch * seq
    nq = seq // bq
    q0 = DSA_WIDTH // LANES
    k0, v0, g0 = q0 + FOX_HEADS, q0 + 2 * FOX_HEADS, q0 + 3 * FOX_HEADS
    return pl.pallas_call(
        functools.partial(_fox_kernel, bq=bq, scale=FOX_DIM ** -0.5),
        grid=(batch, FOX_HEADS, nq),
        in_specs=[pl.BlockSpec((bq, LANES), lambda b, h, i: (b * nq + i, q0 + h)),
                  pl.BlockSpec((seq, LANES), lambda b, h, i: (b, k0 + h)),
                  pl.BlockSpec((seq, LANES), lambda b, h, i: (b, v0 + h)),
                  pl.BlockSpec((bq, LANES), lambda b, h, i: (b * nq + i, g0 + h)),
                  pl.BlockSpec((bq, LANES), lambda b, h, i: (b * nq + i, 0)),
                  pl.BlockSpec((1, seq // FLASH_SUB, FLASH_SUB), lambda b, h, i: (b * FOX_HEADS + h, 0, 0))],
        out_specs=pl.BlockSpec((bq, FOX_DIM), lambda b, h, i: (b * nq + i, h)),
        out_shape=jax.ShapeDtypeStruct((t, FOX_WIDTH), BF16),
        compiler_params=_params("arbitrary", "arbitrary", "arbitrary"),
        name="fox_attention",
    )(proj, proj, proj, proj, cum, cum_t)


def _logf_cumsum_kernel(f_ref, b_ref, o_ref, carry_ref, *, tb):
    @pl.when(pl.program_id(1) == 0)
    def _():
        carry_ref[...] = jnp.zeros_like(carry_ref)

    x = f_ref[...] + b_ref[...]
    logf = jnp.minimum(x, 0.0) - jnp.log1p(jnp.exp(-jnp.abs(x)))
    row = lax.broadcasted_iota(jnp.int32, (tb, tb), 0)
    col = lax.broadcasted_iota(jnp.int32, (tb, tb), 1)
    tri = jnp.where(row >= col, 1.0, 0.0).astype(BF16)
    hi = logf.astype(BF16)
    r1 = logf - hi.astype(F32)
    mid = r1.astype(BF16)
    lo = (r1 - mid.astype(F32)).astype(BF16)
    cum = (jnp.dot(tri, hi, preferred_element_type=F32) + jnp.dot(tri, mid, preferred_element_type=F32)
           + jnp.dot(tri, lo, preferred_element_type=F32)) + carry_ref[...]
    o_ref[...] = cum
    carry_ref[...] = cum[tb - 1:tb, :]


def _logf_cumsum(small, forget_bias, batch, seq, *, tb=512):
    t = batch * seq
    nb = seq // tb
    bias = jnp.zeros((1, LANES), F32).at[0, FOX_F_LANE:FOX_F_LANE + FOX_HEADS].set(forget_bias.astype(F32))
    return pl.pallas_call(
        functools.partial(_logf_cumsum_kernel, tb=tb),
        grid=(batch, nb),
        in_specs=[pl.BlockSpec((tb, LANES), lambda b, i: (b * nb + i, 2)),
                  pl.BlockSpec((1, LANES), lambda b, i: (0, 0))],
        out_specs=pl.BlockSpec((tb, LANES), lambda b, i: (b * nb + i, 0)),
        out_shape=jax.ShapeDtypeStruct((t, LANES), F32),
        scratch_shapes=[pltpu.VMEM((1, LANES), F32)],
        compiler_params=_params("arbitrary", "arbitrary"),
        name="logf_cumsum",
    )(small, bias)


DSA_CK = 512
DSA_BQ = 256
DSA_GROUP = 2
DSA_MAX_ITERS = 640
DSA_PROBES_PER_CHECK = 4


def _row_total(x):
    return jnp.broadcast_to(jnp.sum(x, axis=1, keepdims=True), x.shape)


def _tile_lanes(x, n):
    return jnp.concatenate([x] * n, axis=1)


def _dsa_kernel(qc_ref, qi_ref, wq_ref, kv_ref, gate_ref, cdq_ref, sdq_ref, cdk_ref, sdk_ref, ciq_ref, siq_ref,
                cik_ref, sik_ref, o_ref, kc_scr, vc_scr, ki_scr, idx_scr, q_scr, qi_scr, w_scr, *, seq, topk):
    i = pl.program_id(1)
    bq, ck = DSA_BQ, DSA_CK
    dhalf = DSA_DIM // ROPE_FRACTION // 2
    ihalf = IDX_DIM // ROPE_FRACTION // 2
    rows = DSA_HEADS * bq
    lane = lax.broadcasted_iota(jnp.int32, (bq, LANES), 1)

    @pl.when(i == 0)
    def _():
        tb = 512

        def prep(r, _):
            off = pl.multiple_of(r * tb, tb)
            sl = pl.ds(off, tb)
            kc_scr[sl, :] = _rope(kv_ref[sl, 0:LANES], cdk_ref[sl, :], sdk_ref[sl, :], dhalf, LANES).astype(BF16)
            vc_scr[sl, :LANES] = kv_ref[sl, LANES:2 * LANES].astype(BF16)
            vc_scr[sl, LANES:] = jnp.ones((tb, LANES), BF16)
            lane_t = lax.broadcasted_iota(jnp.int32, (tb, LANES), 1)
            ki = jnp.where(lane_t < IDX_DIM, kv_ref[sl, 2 * LANES:3 * LANES], 0.0)
            ki = _rope(ki, cik_ref[sl, :], sik_ref[sl, :], ihalf, IDX_DIM)
            ki_scr[sl, :] = (ki + pltpu.roll(ki, IDX_DIM, axis=1)).astype(BF16)
            return 0

        lax.fori_loop(0, seq // tb, prep, 0)

    for h in range(DSA_HEADS):
        xh = qc_ref[:, h * LANES:(h + 1) * LANES].astype(F32)
        q_scr[h * bq:(h + 1) * bq, :] = _rope(xh, cdq_ref[...], sdq_ref[...], dhalf, LANES).astype(BF16)
    for p in range(IDX_HEADS // 2):
        xp = _rope(qi_ref[:, p * LANES:(p + 1) * LANES].astype(F32), ciq_ref[...], siq_ref[...], ihalf, IDX_DIM)
        qi_scr[(2 * p) * bq:(2 * p + 1) * bq, :] = jnp.where(lane < IDX_DIM, xp, 0.0).astype(BF16)
        qi_scr[(2 * p + 1) * bq:(2 * p + 2) * bq, :] = jnp.where(lane >= IDX_DIM, xp, 0.0).astype(BF16)
    w = wq_ref[:, 2 * LANES:3 * LANES] * ((IDX_HEADS ** -0.5) * (IDX_DIM ** -0.5))
    for h in range(IDX_HEADS):
        w_scr[h * bq:(h + 1) * bq, :] = _row_total(jnp.where(lane == IDX_W_LANE + h, w, 0.0))

    nch = i // (ck // bq) + 1
    qpos = i * bq + lax.broadcasted_iota(jnp.int32, (bq, LANES), 0)
    inf = jnp.full((bq, LANES), jnp.inf, F32)

    def idx_step(j, carry):
        mn, mx = carry
        off = pl.multiple_of(j * ck, ck)
        r = lax.dot_general(qi_scr[...], ki_scr[pl.ds(off, ck), :], _NT, preferred_element_type=F32)
        halves = []
        for c in range(ck // LANES):
            sc = None
            for h in range(IDX_HEADS):
                term = jnp.maximum(r[h * bq:(h + 1) * bq, c * LANES:(c + 1) * LANES], 0.0) * w_scr[h * bq:(h + 1) * bq, :]
                sc = term if sc is None else sc + term
            valid = (off + c * LANES + lane) <= qpos
            halves.append(jnp.where(valid, sc, -inf))
            mn = jnp.minimum(mn, jnp.where(valid, sc, inf))
            mx = jnp.maximum(mx, jnp.where(valid, sc, -inf))
        idx_scr[j] = jnp.concatenate(halves, axis=1)
        return mn, mx

    mn, mx = lax.fori_loop(0, nch, idx_step, (inf, -inf))
    mn = jnp.broadcast_to(jnp.min(mn, axis=1, keepdims=True), mn.shape)
    mx = jnp.broadcast_to(jnp.max(mx, axis=1, keepdims=True), mx.shape)

    def count_ge(t):
        nslab = bq // LANES

        def body(j, cs):
            out = []
            for u in range(nslab):
                rs = slice(u * LANES, (u + 1) * LANES)
                x = idx_scr[j, rs, :]
                c = cs[u]
                for half in range(ck // LANES):
                    c = c + jnp.where(x[:, half * LANES:(half + 1) * LANES] >= t[rs], 1.0, 0.0)
                out.append(c)
            return tuple(out)

        cs = lax.fori_loop(0, nch, body, tuple(jnp.zeros((LANES, LANES), F32) for _ in range(nslab)))
        return _row_total(jnp.concatenate(cs, axis=0))

    kf = float(topk)
    nvalid = (qpos + 1).astype(F32)
    c_mx = count_ge(mx)
    few = nvalid <= kf
    top_tie = (~few) & (c_mx >= kf)
    lo = jnp.where(top_tie, mx, mn)
    c_lo = jnp.where(top_tie, c_mx, nvalid)
    hi = jnp.where(top_tie, inf, mx)
    c_hi = jnp.where(top_tie, 0.0, c_mx)
    done = jnp.where(few | top_tie | (c_lo == kf), 1.0, 0.0)

    def search_cond(st):
        it, _, _, _, _, done = st
        return jnp.logical_and(it < DSA_MAX_ITERS, jnp.min(done) < 0.5)

    def probe(st):
        lo, hi, c_lo, c_hi, done = st
        mid = 0.5 * lo + 0.5 * hi
        adjacent = (mid <= lo) | (mid >= hi)
        c = count_ge(mid)
        ge = c >= kf
        upd = (done < 0.5) & (~adjacent)
        lo = jnp.where(upd & ge, mid, lo)
        c_lo = jnp.where(upd & ge, c, c_lo)
        hi = jnp.where(upd & (~ge), mid, hi)
        c_hi = jnp.where(upd & (~ge), c, c_hi)
        done = jnp.where(adjacent | (c_lo == kf), 1.0, done)
        return lo, hi, c_lo, c_hi, done

    def search_body(st):
        it, rest = st[0], st[1:]
        for _ in range(DSA_PROBES_PER_CHECK):
            rest = probe(rest)
        return (it + DSA_PROBES_PER_CHECK,) + tuple(rest)

    _, lo, hi, c_lo, c_hi, _ = lax.while_loop(search_cond, search_body, (jnp.int32(0), lo, hi, c_lo, c_hi, done))

    nl = ck // LANES
    lo_w = _tile_lanes(lo, nl)

    def write_plain():
        def body(j, _):
            idx_scr[j] = jnp.where(idx_scr[j] >= lo_w, 0.0, MASKED)
            return 0
        lax.fori_loop(0, nch, body, 0)

    def write_ties():
        hi_w = _tile_lanes(hi, nl)
        need = _tile_lanes(kf - c_hi, nl)
        urow = lax.broadcasted_iota(jnp.int32, (ck, ck), 0)
        ucol = lax.broadcasted_iota(jnp.int32, (ck, ck), 1)
        upper = jnp.where(urow < ucol, 1.0, 0.0).astype(BF16)

        def body(j, before):
            x = idx_scr[j]
            above = x >= hi_w
            eq = jnp.where((x >= lo_w) & (~above), 1.0, 0.0)
            prefix = jnp.dot(eq.astype(BF16), upper, preferred_element_type=F32) + _tile_lanes(before, nl)
            sel = above | ((eq > 0.5) & (prefix < need))
            idx_scr[j] = jnp.where(sel, 0.0, MASKED)
            tot = eq[:, :LANES]
            for c in range(1, nl):
                tot = tot + eq[:, c * LANES:(c + 1) * LANES]
            return before + _row_total(tot)

        lax.fori_loop(0, nch, body, jnp.zeros((bq, LANES), F32))

    lax.cond(jnp.max(c_lo) > kf, write_ties, write_plain)

    grp = DSA_GROUP * bq
    ngrp = rows // grp

    def att_step(j, carry):
        off = pl.multiple_of(j * ck, ck)
        k = kc_scr[pl.ds(off, ck), :]
        v = vc_scr[pl.ds(off, ck), :]
        bias = jnp.concatenate([idx_scr[j]] * DSA_GROUP, axis=0)
        out = []
        for g in range(ngrp):
            m, acc = carry[g]
            s = lax.dot_general(q_scr[g * grp:(g + 1) * grp, :], k, _NT, preferred_element_type=F32)
            parts = [s[:, c * LANES:(c + 1) * LANES] + bias[:, c * LANES:(c + 1) * LANES] for c in range(nl)]
            mc = parts[0]
            for c in range(1, nl):
                mc = jnp.maximum(mc, parts[c])
            m_new = jnp.maximum(m, jnp.max(mc, axis=1, keepdims=True))
            alpha = jnp.exp(m - m_new)
            p = jnp.concatenate([jnp.exp(x - m_new) for x in parts], axis=1).astype(BF16)
            acc = _tile_lanes(alpha, 2) * acc + jnp.dot(p, v, preferred_element_type=F32)
            out.append((m_new, acc))
        return tuple(out)

    init = tuple((jnp.full((grp, LANES), MASKED, F32), jnp.zeros((grp, 2 * LANES), F32)) for _ in range(ngrp))
    res = lax.fori_loop(0, nch, att_step, init)
    for g in range(ngrp):
        _, acc = res[g]
        o = acc[:, :LANES] / acc[:, LANES:]
        for t in range(DSA_GROUP):
            hl = slice((g * DSA_GROUP + t) * LANES, (g * DSA_GROUP + t + 1) * LANES)
            o_ref[:, hl] = (o[t * bq:(t + 1) * bq] * _silu(gate_ref[:, hl].astype(F32))).astype(o_ref.dtype)


def _dsa_attention(proj, small, dsa_tabs_q, dsa_tabs_k, idx_tabs, batch, seq):
    t = batch * seq
    bq = DSA_BQ
    nq = seq // bq
    topk = min(TOPK_MAX, seq // 4)
    rows = DSA_HEADS * bq
    idx_w = IDX_HEADS * IDX_DIM
    c_dsa_q = 2 * DSA_WIDTH + 3 * FOX_WIDTH
    qrow = lambda b, i: (i, 0)
    full = lambda b, i: (0, 0)
    return pl.pallas_call(
        functools.partial(_dsa_kernel, seq=seq, topk=topk),
        grid=(batch, nq),
        in_specs=[pl.BlockSpec((bq, DSA_WIDTH), lambda b, i: (b * nq + i, c_dsa_q // DSA_WIDTH)),
                  pl.BlockSpec((bq, idx_w), lambda b, i: (b * nq + i, (c_dsa_q + DSA_WIDTH) // idx_w)),
                  pl.BlockSpec((bq, 3 * LANES), lambda b, i: (b * nq + i, 0)),
                  pl.BlockSpec((seq, 3 * LANES), lambda b, i: (b, 0)),
                  pl.BlockSpec((bq, DSA_WIDTH), lambda b, i: (b * nq + i, 0)),
                  pl.BlockSpec((bq, LANES), qrow), pl.BlockSpec((bq, LANES), qrow),
                  pl.BlockSpec((seq, LANES), full), pl.BlockSpec((seq, LANES), full),
                  pl.BlockSpec((bq, LANES), qrow), pl.BlockSpec((bq, LANES), qrow),
                  pl.BlockSpec((seq, LANES), full), pl.BlockSpec((seq, LANES), full)],
        out_specs=pl.BlockSpec((bq, DSA_WIDTH), lambda b, i: (b * nq + i, 0)),
        out_shape=jax.ShapeDtypeStruct((t, DSA_WIDTH), BF16),
        scratch_shapes=[pltpu.VMEM((seq, LANES), BF16), pltpu.VMEM((seq, 2 * LANES), BF16),
                        pltpu.VMEM((seq, LANES), BF16), pltpu.VMEM((seq // DSA_CK, bq, DSA_CK), F32),
                        pltpu.VMEM((rows, LANES), BF16), pltpu.VMEM((rows, LANES), BF16),
                        pltpu.VMEM((rows, LANES), F32)],
        compiler_params=_params("arbitrary", "arbitrary"),
        name="dsa_attention",
    )(proj, proj, small, small, proj, *dsa_tabs_q, *dsa_tabs_k, *idx_tabs, *idx_tabs)


def _even_layer(x2, batch, seq, norm_g, w_in, q_norm_g, w_uq, kv_norm_g, w_ukv, diff_lambda, subln_g, w_out,
                lambda_init, bq):
    wt = w_in.T
    n_lat = MLA_Q_RANK + MLA_KV_RANK + MLA_ROPE
    n_big = 2 * MLA_WIDTH + 3 * DIFF_WIDTH
    n_kv_tile = MLA_KV_RANK + LANES
    uq = w_uq.astype(BF16).reshape(MLA_Q_RANK, MLA_HEADS, MLA_NOPE + MLA_ROPE)
    uq = jnp.concatenate([uq, jnp.zeros((MLA_Q_RANK, MLA_HEADS, 2 * LANES - MLA_NOPE - MLA_ROPE), BF16)], axis=2)
    uq = uq.reshape(MLA_Q_RANK, MLA_HEADS * 2 * LANES)
    ukv = w_ukv.astype(BF16).reshape(MLA_KV_RANK, MLA_HEADS, MLA_NOPE + MLA_V)
    ukv = jnp.concatenate([ukv[:, :, :MLA_NOPE].reshape(MLA_KV_RANK, -1), ukv[:, :, MLA_NOPE:].reshape(MLA_KV_RANK, -1)],
                          axis=1)

    h = _rmsnorm(x2, norm_g, BF16)
    proj = _matmul_nt(h, wt, [(n_lat, n_big)], BF16, tm=1024, tn=1024, name="even_inproj")
    lat_q = _matmul_nt(h, wt, [(0, MLA_Q_RANK)], BF16, tm=1024, tn=MLA_Q_RANK, name="even_latent_q")
    lat_kv = _matmul_nt(h, wt, [(MLA_Q_RANK, n_kv_tile)], BF16, tm=1024, tn=n_kv_tile, name="even_latent_kv")
    q = _norm_matmul(lat_q, 0, q_norm_g, uq, tm=1024, tn=2048, name="mla_q_up")
    kv = _norm_matmul(lat_kv, 0, kv_norm_g, ukv, tm=1024, tn=4096, name="mla_kv_up")

    mla_scale = (MLA_NOPE + MLA_ROPE) ** -0.5
    o_a = _mla_attention(q, kv, lat_kv, proj, _rope_tables(seq, MLA_ROPE, LANES, mla_scale),
                         _rope_tables(seq, MLA_ROPE, LANES), batch, seq, bq=bq)
    rot = DIFF_DIM // ROPE_FRACTION
    o_b = _diff_attention(proj, diff_lambda, subln_g, _rope_tables(seq, rot, LANES, DIFF_DIM ** -0.5),
                          _rope_tables(seq, rot, LANES), batch, seq, lambda_init, bq=min(bq, DIFF_BQ))
    return _outproj(o_a, o_b, w_out, x2, tm=1024, tn=512)


def _odd_layer(x2, batch, seq, norm_g, w_in, forget_bias, w_out, bq):
    wt = w_in.T
    c_dsa_k = DSA_WIDTH
    c_idx_q = c_dsa_k + 2 * DSA_DIM
    c_idx_k = c_idx_q + IDX_HEADS * IDX_DIM
    c_gate_c = c_idx_k + IDX_DIM + IDX_HEADS
    c_fox_f = c_gate_c + DSA_WIDTH + 3 * FOX_WIDTH
    c_gate_d = c_fox_f + FOX_HEADS
    big_rows = [(c_gate_c, DSA_WIDTH + 3 * FOX_WIDTH), (c_gate_d, FOX_WIDTH), (0, DSA_WIDTH),
                (c_idx_q, IDX_HEADS * IDX_DIM)]
    n_small = 2 * DSA_DIM + IDX_DIM + IDX_HEADS + FOX_HEADS
    w_small = jnp.concatenate([wt[c_dsa_k:c_dsa_k + 2 * DSA_DIM], wt[c_idx_k:c_idx_k + IDX_DIM + IDX_HEADS],
                               wt[c_fox_f:c_fox_f + FOX_HEADS],
                               jnp.zeros((3 * LANES - n_small, wt.shape[1]), wt.dtype)], axis=0)

    h = _rmsnorm(x2, norm_g, BF16)
    proj = _matmul_nt(h, wt, big_rows, BF16, tm=1024, tn=1024, name="odd_inproj")
    small = _matmul_nt(h, w_small, [(0, 3 * LANES)], F32, tm=1024, tn=3 * LANES, name="odd_small")

    rot = DSA_DIM // ROPE_FRACTION
    o_c = _dsa_attention(proj, small, _rope_tables(seq, rot, LANES, DSA_DIM ** -0.5), _rope_tables(seq, rot, LANES),
                         _rope_tables(seq, IDX_DIM // ROPE_FRACTION, IDX_DIM), batch, seq)
    cum = _logf_cumsum(small, forget_bias, batch, seq)
    cum_t = cum[:, FOX_F_LANE:FOX_F_LANE + FOX_HEADS].reshape(batch, seq, FOX_HEADS).transpose(0, 2, 1)
    cum_t = cum_t.reshape(batch * FOX_HEADS, seq // FLASH_SUB, FLASH_SUB)
    o_d = _fox_attention(proj, cum, cum_t, batch, seq, bq=bq)
    return _outproj(o_c, o_d, w_out, x2, tm=1024, tn=512)


def kernel(x, even_norm, even_w_in, mla_q_norm, mla_w_uq, mla_kv_norm, mla_w_ukv, diff_lambda, diff_subln, even_w_out, odd_norm, odd_w_in, fox_forget_bias, odd_w_out, final_norm):
    batch, seq, d = x.shape
    bq = min(FLASH_BQ, seq)
    h = x.reshape(batch * seq, d)
    depth = even_norm.shape[0] + odd_norm.shape[0]
    for layer in range(depth):
        i = layer // 2
        if layer % 2 == 0:
            lambda_init = 0.8 - 0.6 * math.exp(-0.3 * layer)
            h = _even_layer(h, batch, seq, even_norm[i], even_w_in[i], mla_q_norm[i], mla_w_uq[i], mla_kv_norm[i],
                            mla_w_ukv[i], diff_lambda[i], diff_subln[i], even_w_out[i], lambda_init, bq)
        else:
            h = _odd_layer(h, batch, seq, odd_norm[i], odd_w_in[i], fox_forget_bias[i], odd_w_out[i], bq)
    return _rmsnorm(h, final_norm, x.dtype).reshape(batch, seq, d)
```
